```python
import math
import jax, jax.numpy as jnp
from jax import lax
import numpy as np

D_MODEL = 1024
BATCH = 32
SEQ = 2048
DEPTH = 4

N_EVEN = (DEPTH + 1) // 2
N_ODD = DEPTH // 2
HEAD_DIM = 64
A_Q_HEADS = D_MODEL // 128
A_KV_HEADS = A_Q_HEADS // 4
A_GROUP = A_Q_HEADS // A_KV_HEADS
A_WIDTH = A_Q_HEADS * HEAD_DIM
KV_WIDTH = A_KV_HEADS * HEAD_DIM
WINDOW = 128
BLOCK = 128
ROPE_THETA = 10000.0
NEG_INF = -1e30
B_GROUPS = D_MODEL // 128
B_GROUP_DIM = 64
B_WIDTH = B_GROUPS * B_GROUP_DIM
CHUNK = 128
IN_WIDTH = A_WIDTH + 2 * KV_WIDTH + 2 * B_WIDTH
MIX_WIDTH = A_WIDTH + B_WIDTH
S5_GROUP = 16
S5_GROUPS = D_MODEL // S5_GROUP
S5_STATE = 64
D_FF = ((8 * D_MODEL // 3 + 127) // 128) * 128
N_EXPERTS = 8
TOP_K = 2
ALPHA = (2.0 * DEPTH) ** 0.25
BETA = (8.0 * DEPTH) ** -0.25
LN_EPS = 1e-5

kernel_name = 'hybrid_window_gqa_gmlp_s5_moe_encoder'


def layer_norm(x, g, b):
    xf = x.astype(jnp.float32)
    mu = xf.mean(-1, keepdims=True)
    var = jnp.square(xf - mu).mean(-1, keepdims=True)
    return ((xf - mu) * lax.rsqrt(var + LN_EPS) * g.astype(jnp.float32) + b.astype(jnp.float32)).astype(x.dtype)


def rope_tables(seq):
    pos = jnp.arange(seq, dtype=jnp.float32)
    inv_freq = ROPE_THETA ** (-jnp.arange(0, HEAD_DIM, 2, dtype=jnp.float32) / HEAD_DIM)
    ang = pos[:, None] * inv_freq[None, :]
    return jnp.cos(ang), jnp.sin(ang)


def apply_rope(x, cos, sin):
    xf = x.astype(jnp.float32)
    x1, x2 = jnp.split(xf, 2, axis=-1)
    c = cos[None, :, None, :]
    s = sin[None, :, None, :]
    return jnp.concatenate([x1 * c - x2 * s, x2 * c + x1 * s], axis=-1).astype(x.dtype)


def window_attention(q, k, v, sink):
    Bsz, S = q.shape[0], q.shape[1]
    nb = S // BLOCK
    qb = q.reshape(Bsz, nb, BLOCK, A_KV_HEADS, A_GROUP, HEAD_DIM)

    def neighbours(t):
        tp = jnp.pad(t, ((0, 0), (BLOCK, BLOCK), (0, 0), (0, 0))).reshape(Bsz, nb + 2, BLOCK, A_KV_HEADS, HEAD_DIM)
        return jnp.concatenate([tp[:, :-2], tp[:, 1:-1], tp[:, 2:]], axis=2)

    kb, vb = neighbours(k), neighbours(v)
    scores = jnp.einsum('bnqhgd,bnshd->bnhgqs', qb, kb, preferred_element_type=jnp.float32) * (HEAD_DIM ** -0.5)
    blk = jnp.arange(nb)[:, None]
    qpos = blk * BLOCK + jnp.arange(BLOCK)[None, :]
    kpos = blk * BLOCK - BLOCK + jnp.arange(3 * BLOCK)[None, :]
    kp = kpos[:, None, :]
    valid = (jnp.abs(kp - qpos[:, :, None]) <= WINDOW) & (kp >= 0) & (kp < S)
    scores = jnp.where(valid[None, :, None, None], scores, NEG_INF)
    sink_l = sink.astype(jnp.float32).reshape(1, 1, A_KV_HEADS, A_GROUP, 1, 1)
    m = jnp.maximum(scores.max(-1, keepdims=True), sink_l)
    p = jnp.exp(scores - m)
    denom = p.sum(-1, keepdims=True) + jnp.exp(sink_l - m)
    probs = (p / denom).astype(v.dtype)
    out = jnp.einsum('bnhgqs,bnshd->bnqhgd', probs, vb)
    return out.reshape(Bsz, S, A_WIDTH)


def spatial_gating(z_u, z_v, w_s, b_s, ln_g, ln_b):
    Bsz, S = z_u.shape[0], z_u.shape[1]
    u = jax.nn.gelu(z_u)
    v = layer_norm(jax.nn.gelu(z_v), ln_g, ln_b)
    vc = v.reshape(Bsz, S // CHUNK, CHUNK, B_GROUPS, B_GROUP_DIM)
    sv = jnp.einsum('gpq,bnqgd->bnpgd', w_s, vc) + b_s.T[None, None, :, :, None]
    return (u * sv.reshape(Bsz, S, B_GROUPS, B_GROUP_DIM)).reshape(Bsz, S, B_WIDTH)


def mixer_ab(x, w_in, sink, w_s, b_s, sgu_g, sgu_b, w_out, cos, sin):
    Bsz, S, _ = x.shape
    h = x @ w_in
    q, k, v, z_u, z_v = jnp.split(h, [A_WIDTH, A_WIDTH + KV_WIDTH, A_WIDTH + 2 * KV_WIDTH,
                                      A_WIDTH + 2 * KV_WIDTH + B_WIDTH], axis=-1)
    q = apply_rope(q.reshape(Bsz, S, A_Q_HEADS, HEAD_DIM), cos, sin)
    k = apply_rope(k.reshape(Bsz, S, A_KV_HEADS, HEAD_DIM), cos, sin)
    v = v.reshape(Bsz, S, A_KV_HEADS, HEAD_DIM)
    a_out = window_attention(q, k, v, sink)
    b_out = spatial_gating(z_u.reshape(Bsz, S, B_GROUPS, B_GROUP_DIM), z_v.reshape(Bsz, S, B_GROUPS, B_GROUP_DIM),
                           w_s, b_s, sgu_g, sgu_b)
    return jnp.concatenate([a_out, b_out], axis=-1) @ w_out


def _ssm_combine(left, right):
    a_l, b_l = left
    a_r, b_r = right
    return a_r * a_l, a_r * b_l + b_r


def mixer_c(x, lam_re, lam_im, log_dt, b_re, b_im, c_re, c_im, d_skip, w_val, w_gate):
    Bsz, S, _ = x.shape
    u = x.astype(jnp.float32).reshape(Bsz, S, S5_GROUPS, S5_GROUP)
    uc = u.astype(jnp.complex64)
    y = d_skip.astype(jnp.float32) * u
    for direction, reverse in ((0, False), (1, True)):
        lam = lax.complex(lam_re[direction].astype(jnp.float32), lam_im[direction].astype(jnp.float32))
        dt = jnp.exp(log_dt[direction].astype(jnp.float32))[:, None]
        lam_bar = jnp.exp(lam * dt)
        bmat = lax.complex(b_re[direction].astype(jnp.float32), b_im[direction].astype(jnp.float32))
        b_bar = ((lam_bar - 1.0) / lam)[..., None] * bmat
        bu = jnp.einsum('gpc,bsgc->bsgp', b_bar, uc)
        a = jnp.broadcast_to(lam_bar, (1, S, S5_GROUPS, S5_STATE))
        _, states = lax.associative_scan(_ssm_combine, (a, bu), reverse=reverse, axis=1)
        cmat = lax.complex(c_re[direction].astype(jnp.float32), c_im[direction].astype(jnp.float32))
        y = y + jnp.einsum('gcp,bsgp->bsgc', cmat, states).real
    y = jax.nn.gelu(y.reshape(Bsz, S, D_MODEL)).astype(x.dtype)
    return (y @ w_val) * jax.nn.sigmoid(y @ w_gate)


def swiglu(x, w_gate, w_up, w_down):
    return (jax.nn.silu(x @ w_gate) * (x @ w_up)) @ w_down


def moe_swiglu(x, router_w, w_gate, w_up, w_down):
    Bsz, S, D = x.shape
    t = x.reshape(-1, D)
    logits = jnp.dot(t, router_w, preferred_element_type=jnp.float32)
    top_val, top_idx = lax.top_k(logits, TOP_K)
    top_w = jax.nn.softmax(top_val, axis=-1)
    combine = jnp.einsum('tk,tke->te', top_w, jax.nn.one_hot(top_idx, N_EXPERTS, dtype=jnp.float32))
    out = jnp.zeros_like(t)
    for e in range(N_EXPERTS):
        out = out + combine[:, e:e + 1].astype(t.dtype) * swiglu(t, w_gate[e], w_up[e], w_down[e])
    return out.reshape(Bsz, S, D)


def setup_inputs(seed: int = 0) -> dict:
    key = jax.random.key(seed)
    ks = iter(jax.random.split(key, 32))

    def nrm(shape, scale):
        return scale * jax.random.normal(next(ks), shape, jnp.float32)

    x = nrm((BATCH, SEQ, D_MODEL), 1.0)
    ln_g = 1.0 + nrm((DEPTH, 2, D_MODEL), 0.02)
    ln_b = nrm((DEPTH, 2, D_MODEL), 0.02)
    mix_w_in = nrm((N_EVEN, D_MODEL, IN_WIDTH), D_MODEL ** -0.5)
    attn_sink = nrm((N_EVEN, A_Q_HEADS), 0.5)
    sgu_w = nrm((N_EVEN, B_GROUPS, CHUNK, CHUNK), 0.5 * CHUNK ** -0.5)
    sgu_b = 1.0 + nrm((N_EVEN, B_GROUPS, CHUNK), 0.02)
    sgu_ln_g = 1.0 + nrm((N_EVEN, B_GROUPS, B_GROUP_DIM), 0.02)
    sgu_ln_b = nrm((N_EVEN, B_GROUPS, B_GROUP_DIM), 0.02)
    mix_w_out = nrm((N_EVEN, MIX_WIDTH, D_MODEL), BETA * MIX_WIDTH ** -0.5)
    ffn_w_gate = nrm((N_EVEN, D_MODEL, D_FF), D_MODEL ** -0.5)
    ffn_w_up = nrm((N_EVEN, D_MODEL, D_FF), D_MODEL ** -0.5)
    ffn_w_down = nrm((N_EVEN, D_FF, D_MODEL), BETA * D_FF ** -0.5)
    n = jnp.arange(S5_STATE, dtype=jnp.float32)
    s5_lam_re = -0.5 + nrm((N_ODD, 2, S5_GROUPS, S5_STATE), 0.01)
    s5_lam_im = math.pi * n + nrm((N_ODD, 2, S5_GROUPS, S5_STATE), 0.01)
    s5_log_dt = jax.random.uniform(next(ks), (N_ODD, 2, S5_GROUPS), jnp.float32, math.log(1e-3), math.log(1e-1))
    s5_b_re = nrm((N_ODD, 2, S5_GROUPS, S5_STATE, S5_GROUP), (2 * S5_GROUP) ** -0.5)
    s5_b_im = nrm((N_ODD, 2, S5_GROUPS, S5_STATE, S5_GROUP), (2 * S5_GROUP) ** -0.5)
    s5_c_re = nrm((N_ODD, 2, S5_GROUPS, S5_GROUP, S5_STATE), S5_STATE ** -0.5)
    s5_c_im = nrm((N_ODD, 2, S5_GROUPS, S5_GROUP, S5_STATE), S5_STATE ** -0.5)
    s5_d = nrm((N_ODD, S5_GROUPS, S5_GROUP), 1.0)
    glu_w_val = nrm((N_ODD, D_MODEL, D_MODEL), BETA * D_MODEL ** -0.5)
    glu_w_gate = nrm((N_ODD, D_MODEL, D_MODEL), D_MODEL ** -0.5)
    router_w = nrm((N_ODD, D_MODEL, N_EXPERTS), D_MODEL ** -0.5)
    moe_w_gate = nrm((N_ODD, N_EXPERTS, D_MODEL, D_FF), D_MODEL ** -0.5)
    moe_w_up = nrm((N_ODD, N_EXPERTS, D_MODEL, D_FF), D_MODEL ** -0.5)
    moe_w_down = nrm((N_ODD, N_EXPERTS, D_FF, D_MODEL), BETA * D_FF ** -0.5)
    return {'x': x, 'ln_g': ln_g, 'ln_b': ln_b, 'mix_w_in': mix_w_in, 'attn_sink': attn_sink,
            'sgu_w': sgu_w, 'sgu_b': sgu_b, 'sgu_ln_g': sgu_ln_g, 'sgu_ln_b': sgu_ln_b, 'mix_w_out': mix_w_out,
            'ffn_w_gate': ffn_w_gate, 'ffn_w_up': ffn_w_up, 'ffn_w_down': ffn_w_down,
            's5_lam_re': s5_lam_re, 's5_lam_im': s5_lam_im, 's5_log_dt': s5_log_dt,
            's5_b_re': s5_b_re, 's5_b_im': s5_b_im, 's5_c_re': s5_c_re, 's5_c_im': s5_c_im, 's5_d': s5_d,
            'glu_w_val': glu_w_val, 'glu_w_gate': glu_w_gate, 'router_w': router_w,
            'moe_w_gate': moe_w_gate, 'moe_w_up': moe_w_up, 'moe_w_down': moe_w_down}


def reference(x, ln_g, ln_b, mix_w_in, attn_sink, sgu_w, sgu_b, sgu_ln_g, sgu_ln_b, mix_w_out,
              ffn_w_gate, ffn_w_up, ffn_w_down, s5_lam_re, s5_lam_im, s5_log_dt,
              s5_b_re, s5_b_im, s5_c_re, s5_c_im, s5_d, glu_w_val, glu_w_gate, router_w,
              moe_w_gate, moe_w_up, moe_w_down):
    cos, sin = rope_tables(x.shape[1])
    for layer in range(DEPTH):
        i = layer // 2
        if layer % 2 == 0:
            mixed = mixer_ab(x, mix_w_in[i], attn_sink[i], sgu_w[i], sgu_b[i], sgu_ln_g[i], sgu_ln_b[i],
                             mix_w_out[i], cos, sin)
            x = layer_norm(ALPHA * x + mixed, ln_g[layer, 0], ln_b[layer, 0])
            ffn = swiglu(x, ffn_w_gate[i], ffn_w_up[i], ffn_w_down[i])
            x = layer_norm(ALPHA * x + ffn, ln_g[layer, 1], ln_b[layer, 1])
        else:
            mixed = mixer_c(x, s5_lam_re[i], s5_lam_im[i], s5_log_dt[i], s5_b_re[i], s5_b_im[i],
                            s5_c_re[i], s5_c_im[i], s5_d[i], glu_w_val[i], glu_w_gate[i])
            x = layer_norm(ALPHA * x + mixed, ln_g[layer, 0], ln_b[layer, 0])
            ffn = moe_swiglu(x, router_w[i], moe_w_gate[i], moe_w_up[i], moe_w_down[i])
            x = layer_norm(ALPHA * x + ffn, ln_g[layer, 1], ln_b[layer, 1])
    return x
```

```python
import functools
import math

import jax
import jax.numpy as jnp
from jax import lax
from jax.experimental import pallas as pl
from jax.experimental.pallas import tpu as pltpu

F32 = jnp.float32
BF16 = jnp.bfloat16

LANES = 128
HEAD_DIM = 64
Q_HEADS = 8
KV_HEADS = 2
Q_WIDTH = Q_HEADS * HEAD_DIM
KV_DUP_WIDTH = KV_HEADS * LANES
GATE_WIDTH = 512
GATE_GROUPS = 8
WINDOW_BLOCK = 128
ROPE_THETA = 10000.0
NEG_INF = -1e30
LN_EPS = 1e-5
N_EXPERTS = 8
S5_GROUP = 16
S5_STATE = 64
S5_CHUNK = 64
VMEM_LIMIT = 52 * 1024 * 1024


def _cparams(sem):
    return pltpu.CompilerParams(dimension_semantics=sem, vmem_limit_bytes=VMEM_LIMIT)


def _layer_norm(y, g, b):
    mu = jnp.mean(y, axis=-1, keepdims=True)
    d = y - mu
    var = jnp.mean(d * d, axis=-1, keepdims=True)
    return d * lax.rsqrt(var + LN_EPS) * g + b


def _gelu(x):
    return jax.nn.gelu(x)


def _inproj_kernel(x_ref, w_ref, cos_ref, sin_ref, q_ref, k_ref, v_ref, u_ref, zv_ref):
    xb = x_ref[...].astype(BF16)
    cos = cos_ref[...]
    sin = sin_ref[...]
    lane = lax.broadcasted_iota(jnp.int32, cos.shape, 1)
    first_half = (lane & 32) == 0

    def proj(lo, hi):
        return jnp.dot(xb, w_ref[:, lo:hi], preferred_element_type=F32)

    def rope(blk):
        rot = jnp.where(first_half, pltpu.roll(blk, 96, 1), pltpu.roll(blk, 32, 1))
        return blk * cos + rot * sin

    for j in range(Q_WIDTH // LANES):
        q_ref[:, j * LANES:(j + 1) * LANES] = rope(proj(j * LANES, (j + 1) * LANES)).astype(BF16)
    base = Q_WIDTH
    for j in range(KV_HEADS):
        k_ref[:, j * LANES:(j + 1) * LANES] = rope(proj(base + j * LANES, base + (j + 1) * LANES)).astype(BF16)
    base += KV_DUP_WIDTH
    v_ref[...] = proj(base, base + KV_DUP_WIDTH).astype(BF16)
    base += KV_DUP_WIDTH
    u_ref[...] = _gelu(proj(base, base + GATE_WIDTH)).astype(BF16)
    base += GATE_WIDTH
    zv_ref[...] = _gelu(proj(base, base + GATE_WIDTH)).astype(BF16)


def _inproj(x2d, w_ext, cos_t, sin_t, seq, tm):
    T, D = x2d.shape
    n_ext = w_ext.shape[1]
    per_seq = seq // tm
    row = lambda i: (i, 0)
    return pl.pallas_call(
        _inproj_kernel,
        grid=(T // tm,),
        in_specs=[
            pl.BlockSpec((tm, D), row),
            pl.BlockSpec((D, n_ext), lambda i: (0, 0)),
            pl.BlockSpec((tm, LANES), lambda i: (i % per_seq, 0)),
            pl.BlockSpec((tm, LANES), lambda i: (i % per_seq, 0)),
        ],
        out_specs=[
            pl.BlockSpec((tm, Q_WIDTH), row),
            pl.BlockSpec((tm, KV_DUP_WIDTH), row),
            pl.BlockSpec((tm, KV_DUP_WIDTH), row),
            pl.BlockSpec((tm, GATE_WIDTH), row),
            pl.BlockSpec((tm, GATE_WIDTH), row),
        ],
        out_shape=[
            jax.ShapeDtypeStruct((T, Q_WIDTH), BF16),
            jax.ShapeDtypeStruct((T, KV_DUP_WIDTH), BF16),
            jax.ShapeDtypeStruct((T, KV_DUP_WIDTH), BF16),
            jax.ShapeDtypeStruct((T, GATE_WIDTH), BF16),
            jax.ShapeDtypeStruct((T, GATE_WIDTH), BF16),
        ],
        compiler_params=_cparams(("parallel",)),
        name="inproj_rope_gelu",
    )(x2d, w_ext, cos_t, sin_t)


def _mixer_kernel(sink_ref, q_ref, kp_ref, kc_ref, kn_ref, vp_ref, vc_ref, vn_ref, u_ref, zv_ref,
                  avg_ref, sg_ref, sb_ref, ws_ref, bias_ref, o_ref, kall, vall, *, tiles_per_seq):
    tq = q_ref.shape[0]
    n_blk = tq // WINDOW_BLOCK
    blk = WINDOW_BLOCK
    seq_tile = lax.rem(pl.program_id(0), tiles_per_seq)
    has_prev = seq_tile > 0
    has_next = seq_tile < tiles_per_seq - 1

    kall[0:blk, :] = kp_ref[...]
    kall[blk:blk + tq, :] = kc_ref[...]
    kall[blk + tq:, :] = kn_ref[...]
    vall[0:blk, :] = vp_ref[...]
    vall[blk:blk + tq, :] = vc_ref[...]
    vall[blk + tq:, :] = vn_ref[...]

    qi = lax.broadcasted_iota(jnp.int32, (blk, 3 * blk), 0)
    kj = lax.broadcasted_iota(jnp.int32, (blk, 3 * blk), 1)
    lane = lax.broadcasted_iota(jnp.int32, (blk, LANES), 1)
    low_half = lane < HEAD_DIM
    key_lo_first = jnp.where(has_prev, 0, blk)
    key_hi_last = jnp.where(has_next, 3 * blk, 2 * blk)

    for r in range(n_blk):
        lo = key_lo_first if r == 0 else 0
        hi = key_hi_last if r == n_blk - 1 else 3 * blk
        valid = (kj >= jnp.maximum(qi, lo)) & (kj <= qi + 2 * blk) & (kj < hi)
        rows = slice(r * blk, (r + 1) * blk)
        for h in range(KV_HEADS):
            kd = kall[r * blk:r * blk + 3 * blk, h * LANES:(h + 1) * LANES]
            vd = vall[r * blk:r * blk + 3 * blk, h * LANES:(h + 1) * LANES]
            qa = q_ref[rows, (2 * h) * LANES:(2 * h + 1) * LANES]
            qb = q_ref[rows, (2 * h + 1) * LANES:(2 * h + 2) * LANES]
            zero = jnp.zeros_like(qa)
            lhs = jnp.concatenate([jnp.where(low_half, qa, zero), jnp.where(low_half, zero, qa),
                                   jnp.where(low_half, qb, zero), jnp.where(low_half, zero, qb)], axis=0)
            s = lax.dot_general(lhs, kd, (((1,), (1,)), ((), ())), preferred_element_type=F32)
            s = s * (HEAD_DIM ** -0.5)
            probs = []
            denoms = []
            for g in range(4):
                sink = sink_ref[h * 4 + g]
                sg = jnp.where(valid, s[g * blk:(g + 1) * blk], NEG_INF)
                m = jnp.maximum(jnp.max(sg, axis=-1, keepdims=True), sink)
                p = jnp.exp(sg - m)
                denoms.append(jnp.sum(p, axis=-1, keepdims=True) + jnp.exp(sink - m))
                probs.append(p.astype(BF16))
            pv = jnp.dot(jnp.concatenate(probs, axis=0), vd, preferred_element_type=F32)
            outs = [pv[g * blk:(g + 1) * blk] / denoms[g] for g in range(4)]
            o_ref[rows, (2 * h) * LANES:(2 * h + 1) * LANES] = jnp.where(low_half, outs[0], outs[1]).astype(BF16)
            o_ref[rows, (2 * h + 1) * LANES:(2 * h + 2) * LANES] = jnp.where(low_half, outs[2], outs[3]).astype(BF16)

        vg = zv_ref[rows, :]
        avg = avg_ref[...]
        mean = jnp.dot(vg, avg, preferred_element_type=F32)
        d = vg.astype(F32) - mean
        d2 = d * d
        d2_hi = d2.astype(BF16)
        d2_lo = (d2 - d2_hi.astype(F32)).astype(BF16)
        var = jnp.dot(d2_hi, avg, preferred_element_type=F32) + jnp.dot(d2_lo, avg, preferred_element_type=F32)
        vn = (d * lax.rsqrt(var + LN_EPS) * sg_ref[...] + sb_ref[...]).astype(BF16)
        for m_ in range(GATE_GROUPS // 2):
            cols = slice(m_ * LANES, (m_ + 1) * LANES)
            rhs = vn[:, cols]
            a = jnp.dot(ws_ref[2 * m_], rhs, preferred_element_type=F32)
            b = jnp.dot(ws_ref[2 * m_ + 1], rhs, preferred_element_type=F32)
            sv = jnp.where(low_half, a, b) + bias_ref[:, cols]
            o_ref[rows, Q_WIDTH + m_ * LANES:Q_WIDTH + (m_ + 1) * LANES] = (
                u_ref[rows, cols].astype(F32) * sv).astype(BF16)


def _mixer(q, k, v, u, zv, sink, avg, sgu_g, sgu_b, ws, bias, seq, tq):
    T = q.shape[0]
    r = tq // WINDOW_BLOCK
    tiles_per_seq = seq // tq
    blocks_per_seq = seq // WINDOW_BLOCK

    def prev_map(i, sink_ref):
        first = (i // tiles_per_seq) * blocks_per_seq
        return (jnp.maximum(i * r - 1, first), 0)

    def next_map(i, sink_ref):
        last = (i // tiles_per_seq + 1) * blocks_per_seq - 1
        return (jnp.minimum((i + 1) * r, last), 0)

    row = lambda i, sink_ref: (i, 0)
    const2 = lambda i, sink_ref: (0, 0)
    nb_spec_p = pl.BlockSpec((WINDOW_BLOCK, KV_DUP_WIDTH), prev_map)
    nb_spec_n = pl.BlockSpec((WINDOW_BLOCK, KV_DUP_WIDTH), next_map)
    cur_spec = pl.BlockSpec((tq, KV_DUP_WIDTH), row)
    grid_spec = pltpu.PrefetchScalarGridSpec(
        num_scalar_prefetch=1,
        grid=(T // tq,),
        in_specs=[
            pl.BlockSpec((tq, Q_WIDTH), row),
            nb_spec_p, cur_spec, nb_spec_n,
            nb_spec_p, cur_spec, nb_spec_n,
            pl.BlockSpec((tq, GATE_WIDTH), row),
            pl.BlockSpec((tq, GATE_WIDTH), row),
            pl.BlockSpec((GATE_WIDTH, GATE_WIDTH), const2),
            pl.BlockSpec((1, GATE_WIDTH), const2),
            pl.BlockSpec((1, GATE_WIDTH), const2),
            pl.BlockSpec((GATE_GROUPS, WINDOW_BLOCK, WINDOW_BLOCK), lambda i, sink_ref: (0, 0, 0)),
            pl.BlockSpec((WINDOW_BLOCK, GATE_WIDTH), const2),
        ],
        out_specs=pl.BlockSpec((tq, Q_WIDTH + GATE_WIDTH), row),
        scratch_shapes=[
            pltpu.VMEM((tq + 2 * WINDOW_BLOCK, KV_DUP_WIDTH), BF16),
            pltpu.VMEM((tq + 2 * WINDOW_BLOCK, KV_DUP_WIDTH), BF16),
        ],
    )
    return pl.pallas_call(
        functools.partial(_mixer_kernel, tiles_per_seq=tiles_per_seq),
        grid_spec=grid_spec,
        out_shape=jax.ShapeDtypeStruct((T, Q_WIDTH + GATE_WIDTH), BF16),
        compiler_params=_cparams(("parallel",)),
        name="window_attn_spatial_gate",
    )(sink, q, k, k, k, v, v, v, u, zv, avg, sgu_g, sgu_b, ws, bias)


def _proj_ln_kernel(a_ref, x_ref, w_ref, g_ref, b_ref, o_ref, *, alpha):
    y = jnp.dot(a_ref[...], w_ref[...], preferred_element_type=F32)
    o_ref[...] = _layer_norm(alpha * x_ref[...] + y, g_ref[...], b_ref[...])


def _glu_ln_kernel(a_ref, x_ref, wv_ref, wg_ref, g_ref, b_ref, o_ref, *, alpha):
    a = a_ref[...]
    val = jnp.dot(a, wv_ref[...], preferred_element_type=F32)
    gate = jnp.dot(a, wg_ref[...], preferred_element_type=F32)
    o_ref[...] = _layer_norm(alpha * x_ref[...] + val * jax.nn.sigmoid(gate), g_ref[...], b_ref[...])


def _proj_ln(a, x2d, weights, ln_g, ln_b, alpha, tm):
    T, D = x2d.shape
    K = a.shape[1]
    row = lambda i: (i, 0)
    const = lambda i: (0, 0)
    body = _proj_ln_kernel if len(weights) == 1 else _glu_ln_kernel
    return pl.pallas_call(
        functools.partial(body, alpha=alpha),
        grid=(T // tm,),
        in_specs=[pl.BlockSpec((tm, K), row), pl.BlockSpec((tm, D), row)]
        + [pl.BlockSpec((K, D), const) for _ in weights]
        + [pl.BlockSpec((1, D), const), pl.BlockSpec((1, D), const)],
        out_specs=pl.BlockSpec((tm, D), row),
        out_shape=jax.ShapeDtypeStruct((T, D), F32),
        compiler_params=_cparams(("parallel",)),
        name="proj_residual_ln" if len(weights) == 1 else "glu_residual_ln",
    )(a, x2d, *weights, ln_g, ln_b)


def _swiglu_ln_kernel(x_ref, wg_ref, wu_ref, wd_ref, g_ref, b_ref, o_ref, xb, acc, *, alpha):
    f = pl.program_id(1)

    @pl.when(f == 0)
    def _():
        xb[...] = x_ref[...].astype(BF16)

    xv = xb[...]
    gate = jnp.dot(xv, wg_ref[...], preferred_element_type=F32)
    up = jnp.dot(xv, wu_ref[...], preferred_element_type=F32)
    h = (jax.nn.silu(gate) * up).astype(BF16)
    part = jnp.dot(h, wd_ref[...], preferred_element_type=F32)

    @pl.when(f == 0)
    def _():
        acc[...] = part

    @pl.when(f > 0)
    def _():
        acc[...] += part

    @pl.when(f == pl.num_programs(1) - 1)
    def _():
        o_ref[...] = _layer_norm(alpha * x_ref[...] + acc[...], g_ref[...], b_ref[...])


def _swiglu_ln(x2d, wg, wu, wd, ln_g, ln_b, alpha, tm, tf):
    T, D = x2d.shape
    FF = wg.shape[1]
    return pl.pallas_call(
        functools.partial(_swiglu_ln_kernel, alpha=alpha),
        grid=(T // tm, FF // tf),
        in_specs=[
            pl.BlockSpec((tm, D), lambda i, f: (i, 0)),
            pl.BlockSpec((D, tf), lambda i, f: (0, f)),
            pl.BlockSpec((D, tf), lambda i, f: (0, f)),
            pl.BlockSpec((tf, D), lambda i, f: (f, 0)),
            pl.BlockSpec((1, D), lambda i, f: (0, 0)),
            pl.BlockSpec((1, D), lambda i, f: (0, 0)),
        ],
        out_specs=pl.BlockSpec((tm, D), lambda i, f: (i, 0)),
        out_shape=jax.ShapeDtypeStruct((T, D), F32),
        scratch_shapes=[pltpu.VMEM((tm, D), BF16), pltpu.VMEM((tm, D), F32)],
        compiler_params=_cparams(("parallel", "arbitrary")),
        name="swiglu_residual_ln",
    )(x2d, wg, wu, wd, ln_g, ln_b)


def _s5_kernel(u_ref, wt_ref, ws_ref, wo_ref, are_ref, aim_ref, d_ref, o_ref, gbuf, sbuf, *, n_chunks, bsz):
    P2 = 2 * S5_STATE
    u = u_ref[...]
    ub = u.astype(BF16)
    gbuf[...] = jnp.dot(ub, ws_ref[...], preferred_element_type=F32)
    a_re = are_ref[...]
    a_im = aim_ref[...]
    lane = lax.broadcasted_iota(jnp.int32, (bsz, P2), 1)
    fwd = lane < S5_STATE
    s_re = jnp.zeros((bsz, P2), F32)
    s_im = jnp.zeros((bsz, P2), F32)
    for i in range(n_chunks):
        if i > 0:
            rf = slice((i - 1) * bsz, i * bsz)
            rb = slice((n_chunks - i) * bsz, (n_chunks - i + 1) * bsz)
            g_re = jnp.where(fwd, gbuf[rf, 0:P2], gbuf[rb, 0:P2])
            g_im = jnp.where(fwd, gbuf[rf, P2:2 * P2], gbuf[rb, P2:2 * P2])
            s_re, s_im = (a_re * s_re - a_im * s_im + g_re, a_re * s_im + a_im * s_re + g_im)
        of = slice(i * bsz, (i + 1) * bsz)
        ob = slice((n_chunks - 1 - i) * bsz, (n_chunks - i) * bsz)
        sbuf[of, 0:S5_STATE] = s_re[:, 0:S5_STATE]
        sbuf[ob, S5_STATE:P2] = s_re[:, S5_STATE:P2]
        sbuf[of, P2:P2 + S5_STATE] = s_im[:, 0:S5_STATE]
        sbuf[ob, P2 + S5_STATE:2 * P2] = s_im[:, S5_STATE:P2]
    y = jnp.dot(ub, wt_ref[...], preferred_element_type=F32)
    y = y + jnp.dot(sbuf[...].astype(BF16), wo_ref[...], preferred_element_type=F32)
    y = y + d_ref[...] * u
    o_ref[...] = _gelu(y).astype(BF16)


def _s5_tables(lam_re, lam_im, log_dt, b_re, b_im, c_re, c_im, d_skip):
    L = S5_CHUNK
    hp = lax.Precision.HIGHEST
    G, P = lam_re.shape[1], lam_re.shape[2]
    C = b_re.shape[-1]
    dt = jnp.exp(log_dt)[..., None]
    zr, zi = lam_re * dt, lam_im * dt
    er = jnp.exp(zr)
    lbr, lbi = er * jnp.cos(zi), er * jnp.sin(zi)
    nr, ni = lbr - 1.0, lbi
    den = lam_re * lam_re + lam_im * lam_im
    cr = (nr * lam_re + ni * lam_im) / den
    ci = (ni * lam_re - nr * lam_im) / den
    bbr = cr[..., None] * b_re - ci[..., None] * b_im
    bbi = cr[..., None] * b_im + ci[..., None] * b_re
    k = jnp.arange(L + 1, dtype=F32)[:, None, None, None]
    mag = jnp.exp(k * zr[None])
    pwr, pwi = mag * jnp.cos(k * zi[None]), mag * jnp.sin(k * zi[None])

    cbr = c_re[..., None] * bbr[:, :, None] - c_im[..., None] * bbi[:, :, None]
    cbi = c_re[..., None] * bbi[:, :, None] + c_im[..., None] * bbr[:, :, None]
    kern = (jnp.einsum('kzgp,zgcpd->zgkcd', pwr[:L], cbr, precision=hp)
            - jnp.einsum('kzgp,zgcpd->zgkcd', pwi[:L], cbi, precision=hp))
    kf, kb = kern[0], kern[1]
    kfull = jnp.concatenate([kb[:, 1:][:, ::-1], (kf[:, 0] + kb[:, 0])[:, None], kf[:, 1:]], axis=1)
    tau = jnp.arange(L)[:, None]
    t = jnp.arange(L)[None, :]
    toep = kfull[:, t - tau + L - 1]
    w_toep = toep.transpose(0, 1, 4, 2, 3).reshape(G, L * C, L * C)

    pf_r, pf_i = pwr[:L, 0][::-1], pwi[:L, 0][::-1]
    pb_r, pb_i = pwr[:L, 1], pwi[:L, 1]

    def state_cols(pr, pi, br, bi):
        re = pr[..., None] * br[None] - pi[..., None] * bi[None]
        im = pr[..., None] * bi[None] + pi[..., None] * br[None]
        return re.transpose(1, 0, 3, 2), im.transpose(1, 0, 3, 2)

    sf_r, sf_i = state_cols(pf_r, pf_i, bbr[0], bbi[0])
    sb_r, sb_i = state_cols(pb_r, pb_i, bbr[1], bbi[1])
    w_state = jnp.concatenate([sf_r, sb_r, sf_i, sb_i], axis=-1).reshape(G, L * C, 4 * P)

    def out_rows(pr, pi, cre, cim):
        er_ = cre[None] * pr[:, :, None] - cim[None] * pi[:, :, None]
        ei_ = cre[None] * pi[:, :, None] + cim[None] * pr[:, :, None]
        return er_.transpose(1, 3, 0, 2), -ei_.transpose(1, 3, 0, 2)

    of_r, of_i = out_rows(pwr[1:L + 1, 0], pwi[1:L + 1, 0], c_re[0], c_im[0])
    ob_r, ob_i = out_rows(pwr[1:L + 1, 1][::-1], pwi[1:L + 1, 1][::-1], c_re[1], c_im[1])
    w_out = jnp.concatenate([of_r, ob_r, of_i, ob_i], axis=1).reshape(G, 4 * P, L * C)

    a_re = jnp.concatenate([pwr[L, 0], pwr[L, 1]], axis=-1)[:, None, :]
    a_im = jnp.concatenate([pwi[L, 0], pwi[L, 1]], axis=-1)[:, None, :]
    d_row = jnp.tile(d_skip[:, None, :], (1, L, 1)).reshape(G, 1, L * C)
    return w_toep.astype(BF16), w_state.astype(BF16), w_out.astype(BF16), a_re, a_im, d_row


def _s5(x3d, tables):
    w_toep, w_state, w_out, a_re, a_im, d_row = tables
    Bsz, S, D = x3d.shape
    L, C = S5_CHUNK, S5_GROUP
    G = D // C
    n = S // L
    R = n * Bsz
    W = L * C
    u = x3d.reshape(Bsz, n, L, G, C).transpose(3, 1, 0, 2, 4).reshape(G, R, W)
    grp = lambda g: (g, 0, 0)
    y = pl.pallas_call(
        functools.partial(_s5_kernel, n_chunks=n, bsz=Bsz),
        grid=(G,),
        in_specs=[
            pl.BlockSpec((None, R, W), grp),
            pl.BlockSpec((None, W, W), grp),
            pl.BlockSpec((None, W, 4 * S5_STATE), grp),
            pl.BlockSpec((None, 4 * S5_STATE, W), grp),
            pl.BlockSpec((None, 1, 2 * S5_STATE), grp),
            pl.BlockSpec((None, 1, 2 * S5_STATE), grp),
            pl.BlockSpec((None, 1, W), grp),
        ],
        out_specs=pl.BlockSpec((None, R, W), grp),
        out_shape=jax.ShapeDtypeStruct((G, R, W), BF16),
        scratch_shapes=[pltpu.VMEM((R, 4 * S5_STATE), F32), pltpu.VMEM((R, 4 * S5_STATE), F32)],
        compiler_params=_cparams(("parallel",)),
        name="s5_chunked_conv",
    )(u, w_toep, w_state, w_out, a_re, a_im, d_row)
    return y.reshape(G, n, Bsz, L, C).transpose(2, 1, 3, 0, 4).reshape(Bsz * S, D)


def _router_kernel(x_ref, wh_ref, wl_ref, o_ref):
    x = x_ref[...]
    xh = x.astype(BF16)
    xl = (x - xh.astype(F32)).astype(BF16)
    wh = wh_ref[...]
    logits = (jnp.dot(xh, wh, preferred_element_type=F32) + jnp.dot(xl, wh, preferred_element_type=F32)
              + jnp.dot(xh, wl_ref[...], preferred_element_type=F32))
    lane = lax.broadcasted_iota(jnp.int32, logits.shape, 1)
    logits = jnp.where(lane < N_EXPERTS, logits, -jnp.inf)
    v1 = jnp.max(logits, axis=-1, keepdims=True)
    lane_f = lane.astype(F32)
    i1 = jnp.min(jnp.where(logits == v1, lane_f, float(LANES)), axis=-1, keepdims=True)
    rest = jnp.where(lane_f == i1, -jnp.inf, logits)
    v2 = jnp.max(rest, axis=-1, keepdims=True)
    i2 = jnp.min(jnp.where(rest == v2, lane_f, float(LANES)), axis=-1, keepdims=True)
    e2 = jnp.exp(v2 - v1)
    den = 1.0 + e2
    w1 = 1.0 / den
    w2 = e2 / den
    o_ref[...] = jnp.where(lane == 0, w1, jnp.where(lane == 1, w2, jnp.where(
        lane == 2, i1, jnp.where(lane == 3, i2, 0.0))))


def _router(x2d, router_w, tm):
    T, D = x2d.shape
    wpad = jnp.zeros((D, LANES), F32).at[:, :N_EXPERTS].set(router_w)
    wh = wpad.astype(BF16)
    wl = (wpad - wh.astype(F32)).astype(BF16)
    return pl.pallas_call(
        _router_kernel,
        grid=(T // tm,),
        in_specs=[pl.BlockSpec((tm, D), lambda i: (i, 0)),
                  pl.BlockSpec((D, LANES), lambda i: (0, 0)),
                  pl.BlockSpec((D, LANES), lambda i: (0, 0))],
        out_specs=pl.BlockSpec((tm, LANES), lambda i: (i, 0)),
        out_shape=jax.ShapeDtypeStruct((T, LANES), F32),
        compiler_params=_cparams(("parallel",)),
        name="router_top2",
    )(x2d, wh, wl)


def _route_plan(info, tm):
    T = info.shape[0]
    n_pairs = 2 * T
    n_tiles = n_pairs // tm + N_EXPERTS
    n_slots = n_tiles * tm
    gate_w = info[:, 0:2].T.reshape(-1)
    expert = info[:, 2:4].astype(jnp.int32).T.reshape(-1)
    order = jnp.sort(expert * n_pairs + jnp.arange(n_pairs, dtype=jnp.int32)) % n_pairs
    counts = jnp.sum(expert[:, None] == jnp.arange(N_EXPERTS, dtype=jnp.int32)[None, :], axis=0, dtype=jnp.int32)
    padded = ((counts + tm - 1) // tm) * tm
    pad_end = jnp.cumsum(padded)
    pad_start = pad_end - padded
    cmp_start = jnp.cumsum(counts) - counts
    tile_start = jnp.arange(n_tiles, dtype=jnp.int32) * tm
    tile_expert = jnp.minimum(jnp.sum(tile_start[:, None] >= pad_end[None, :], axis=1, dtype=jnp.int32),
                              N_EXPERTS - 1)
    n_valid = jnp.clip(counts[tile_expert] - (tile_start - pad_start[tile_expert]), 0, tm)
    n_valid = jnp.where(tile_start < pad_end[-1], n_valid, 0).astype(jnp.int32)
    slot = jnp.arange(n_slots, dtype=jnp.int32)
    s_exp = jnp.repeat(tile_expert, tm)
    rank = slot - pad_start[s_exp]
    valid = (rank < counts[s_exp]) & (slot < pad_end[-1])
    pair = order[jnp.clip(cmp_start[s_exp] + rank, 0, n_pairs - 1)]
    slot_tok = jnp.where(valid, pair % T, 0)
    slot_dst = jnp.where(valid, pair, 0)
    slot_w = jnp.where(valid, gate_w[pair], 0.0)
    return (tile_expert, n_valid, slot_tok.reshape(n_tiles, 1, tm), slot_dst.reshape(n_tiles, 1, tm),
            slot_w.reshape(n_slots, 1), n_tiles)


def _moe_kernel(texp_ref, nvalid_ref, tok0_ref, tokc_ref, tokn_ref, dst_ref, x_hbm, sw_ref, wg_ref, wu_ref, wd_ref,
                out_hbm, xbuf, xb, acc, obuf, sem_in, sem_out):
    i = pl.program_id(0)
    f = pl.program_id(1)
    n_f = pl.num_programs(1)
    n_tiles = pl.num_programs(0)
    tm = xb.shape[0]
    rows_here = nvalid_ref[i]
    rows_prev = nvalid_ref[jnp.maximum(i - 1, 0)]
    active = rows_here > 0
    next_active = jnp.logical_and(i + 1 < n_tiles, nvalid_ref[jnp.minimum(i + 1, n_tiles - 1)] > 0)
    slot = lax.rem(i, 2)

    def gather_copy(tok_ref, r, buf_slot):
        return pltpu.make_async_copy(x_hbm.at[pl.ds(tok_ref[0, 0, r], 1), :],
                                     xbuf.at[buf_slot, pl.ds(r, 1), :], sem_in.at[buf_slot])

    def scatter_copy(r):
        return pltpu.make_async_copy(obuf.at[pl.ds(r, 1), :],
                                     out_hbm.at[pl.ds(dst_ref[0, 0, r], 1), :], sem_out.at[0])

    def start_gather(tok_ref, buf_slot):
        def body(r, c):
            gather_copy(tok_ref, r, buf_slot).start()
            return c
        lax.fori_loop(0, tm, body, 0)

    def wait_gather(tok_ref, buf_slot):
        def body(r, c):
            gather_copy(tok_ref, r, buf_slot).wait()
            return c
        lax.fori_loop(0, tm, body, 0)

    def wait_scatter(n_rows):
        def body(r, c):
            scatter_copy(r).wait()
            return c
        lax.fori_loop(0, n_rows, body, 0)

    @pl.when(jnp.logical_and(f == 0, i == 0))
    def _():
        start_gather(tok0_ref, 0)

    @pl.when(jnp.logical_and(f == 0, active))
    def _():
        wait_gather(tokc_ref, slot)
        xb[...] = xbuf[slot].astype(BF16)

        @pl.when(next_active)
        def _():
            start_gather(tokn_ref, 1 - slot)

    @pl.when(active)
    def _():
        xv = xb[...]
        gate = jnp.dot(xv, wg_ref[...], preferred_element_type=F32)
        up = jnp.dot(xv, wu_ref[...], preferred_element_type=F32)
        h = (jax.nn.silu(gate) * up).astype(BF16)
        part = jnp.dot(h, wd_ref[...], preferred_element_type=F32)

        @pl.when(f == 0)
        def _():
            acc[...] = part

        @pl.when(f > 0)
        def _():
            acc[...] += part

    @pl.when(jnp.logical_and(f == n_f - 1, active))
    def _():
        @pl.when(i > 0)
        def _():
            wait_scatter(rows_prev)

        obuf[...] = acc[...] * sw_ref[...]

        def body(r, c):
            scatter_copy(r).start()
            return c
        lax.fori_loop(0, rows_here, body, 0)

        @pl.when(jnp.logical_not(next_active))
        def _():
            wait_scatter(rows_here)


def _moe_experts(x2d, plan, wg, wu, wd, tm, tf):
    tile_expert, n_valid, slot_tok, slot_dst, slot_w, n_tiles = plan
    T, D = x2d.shape
    FF = wg.shape[2]
    smem_blk = lambda imap: pl.BlockSpec((1, 1, tm), imap, memory_space=pltpu.SMEM)
    grid_spec = pltpu.PrefetchScalarGridSpec(
        num_scalar_prefetch=2,
        grid=(n_tiles, FF // tf),
        in_specs=[
            smem_blk(lambda i, f, te, nu: (0, 0, 0)),
            smem_blk(lambda i, f, te, nu: (i, 0, 0)),
            smem_blk(lambda i, f, te, nu: (jnp.minimum(i + 1, n_tiles - 1), 0, 0)),
            smem_blk(lambda i, f, te, nu: (i, 0, 0)),
            pl.BlockSpec(memory_space=pl.ANY),
            pl.BlockSpec((tm, 1), lambda i, f, te, nu: (i, 0)),
            pl.BlockSpec((None, D, tf), lambda i, f, te, nu: (te[i], 0, f)),
            pl.BlockSpec((None, D, tf), lambda i, f, te, nu: (te[i], 0, f)),
            pl.BlockSpec((None, tf, D), lambda i, f, te, nu: (te[i], f, 0)),
        ],
        out_specs=pl.BlockSpec(memory_space=pl.ANY),
        scratch_shapes=[
            pltpu.VMEM((2, tm, D), F32),
            pltpu.VMEM((tm, D), BF16),
            pltpu.VMEM((tm, D), F32),
            pltpu.VMEM((tm, D), F32),
            pltpu.SemaphoreType.DMA((2,)),
            pltpu.SemaphoreType.DMA((1,)),
        ],
    )
    return pl.pallas_call(
        _moe_kernel,
        grid_spec=grid_spec,
        out_shape=jax.ShapeDtypeStruct((2 * T, D), F32),
        compiler_params=_cparams(("arbitrary", "arbitrary")),
        name="moe_grouped_swiglu",
    )(tile_expert, n_valid, slot_tok, slot_tok, slot_tok, slot_dst, x2d, slot_w, wg, wu, wd)


def _combine_ln_kernel(x_ref, y0_ref, y1_ref, g_ref, b_ref, o_ref, *, alpha):
    o_ref[...] = _layer_norm(alpha * x_ref[...] + (y0_ref[...] + y1_ref[...]), g_ref[...], b_ref[...])


def _combine_ln(x2d, y, ln_g, ln_b, alpha, tm):
    T, D = x2d.shape
    nt = T // tm
    return pl.pallas_call(
        functools.partial(_combine_ln_kernel, alpha=alpha),
        grid=(nt,),
        in_specs=[pl.BlockSpec((tm, D), lambda i: (i, 0)),
                  pl.BlockSpec((tm, D), lambda i: (i, 0)),
                  pl.BlockSpec((tm, D), lambda i: (i + nt, 0)),
                  pl.BlockSpec((1, D), lambda i: (0, 0)),
                  pl.BlockSpec((1, D), lambda i: (0, 0))],
        out_specs=pl.BlockSpec((tm, D), lambda i: (i, 0)),
        out_shape=jax.ShapeDtypeStruct((T, D), F32),
        compiler_params=_cparams(("parallel",)),
        name="moe_combine_ln",
    )(x2d, y, y, ln_g, ln_b)


def _rope_tables(seq):
    pos = jnp.arange(seq, dtype=F32)
    inv_freq = ROPE_THETA ** (-jnp.arange(0, HEAD_DIM, 2, dtype=F32) / HEAD_DIM)
    ang = pos[:, None] * inv_freq[None, :]
    cos, sin = jnp.cos(ang), jnp.sin(ang)
    return jnp.tile(cos, (1, 4)), jnp.tile(jnp.concatenate([-sin, sin], axis=1), (1, 2))


def kernel(x, ln_g, ln_b, mix_w_in, attn_sink, sgu_w, sgu_b, sgu_ln_g, sgu_ln_b, mix_w_out, ffn_w_gate, ffn_w_up, ffn_w_down, s5_lam_re, s5_lam_im, s5_log_dt, s5_b_re, s5_b_im, s5_c_re, s5_c_im, s5_d, glu_w_val, glu_w_gate, router_w, moe_w_gate, moe_w_up, moe_w_down):
    Bsz, S, D = x.shape
    depth = ln_g.shape[0]
    T = Bsz * S
    alpha = (2.0 * depth) ** 0.25
    tm = 512
    cos_t, sin_t = _rope_tables(S)
    avg = jnp.kron(jnp.eye(GATE_GROUPS, dtype=F32), jnp.full((HEAD_DIM, HEAD_DIM), 1.0 / HEAD_DIM, F32)).astype(BF16)
    x2d = x.reshape(T, D)
    for layer in range(depth):
        i = layer // 2
        g0, b0 = ln_g[layer, 0][None, :], ln_b[layer, 0][None, :]
        g1, b1 = ln_g[layer, 1][None, :], ln_b[layer, 1][None, :]
        if layer % 2 == 0:
            w = mix_w_in[i]
            wq, wk, wv, wz = (w[:, :Q_WIDTH], w[:, Q_WIDTH:Q_WIDTH + 128], w[:, Q_WIDTH + 128:Q_WIDTH + 256],
                              w[:, Q_WIDTH + 256:])
            dup = lambda m: jnp.concatenate([m[:, :HEAD_DIM], m[:, :HEAD_DIM], m[:, HEAD_DIM:], m[:, HEAD_DIM:]], axis=1)
            w_ext = jnp.concatenate([wq, dup(wk), dup(wv), wz], axis=1).astype(BF16)
            q, k, v, u, zv = _inproj(x2d, w_ext, cos_t, sin_t, S, tm)
            bias = jnp.repeat(sgu_b[i].T, HEAD_DIM, axis=1)
            mixed = _mixer(q, k, v, u, zv, attn_sink[i], avg, sgu_ln_g[i].reshape(1, -1),
                           sgu_ln_b[i].reshape(1, -1), sgu_w[i].astype(BF16), bias, S, tm)
            x2d = _proj_ln(mixed, x2d, [mix_w_out[i].astype(BF16)], g0, b0, alpha, tm)
            x2d = _swiglu_ln(x2d, ffn_w_gate[i].astype(BF16), ffn_w_up[i].astype(BF16),
                             ffn_w_down[i].astype(BF16), g1, b1, alpha, 1024, 256)
        else:
            tables = _s5_tables(s5_lam_re[i], s5_lam_im[i], s5_log_dt[i], s5_b_re[i], s5_b_im[i],
                                s5_c_re[i], s5_c_im[i], s5_d[i])
            y = _s5(x2d.reshape(Bsz, S, D), tables)
            x2d = _proj_ln(y, x2d, [glu_w_val[i].astype(BF16), glu_w_gate[i].astype(BF16)], g0, b0, alpha, tm)
            info = _router(x2d, router_w[i], tm)
            plan = _route_plan(info, tm)
            y = _moe_experts(x2d, plan, moe_w_gate[i].astype(BF16), moe_w_up[i].astype(BF16),
                             moe_w_down[i].astype(BF16), tm, 256)
            x2d = _combine_ln(x2d, y, g1, b1, alpha, tm)
    return x2d.reshape(Bsz, S, D)
```

```python
import functools
import math

import jax
import jax.numpy as jnp
from jax import lax
from jax.experimental import pallas as pl
from jax.experimental.pallas import tpu as pltpu

F32 = jnp.float32
BF16 = jnp.bfloat16

LANES = 128
HEAD_DIM = 64
Q_HEADS = 8
KV_HEADS = 2
Q_WIDTH = Q_HEADS * HEAD_DIM
KV_DUP_WIDTH = KV_HEADS * LANES
GATE_WIDTH = 512
GATE_GROUPS = 8
WINDOW_BLOCK = 128
ROPE_THETA = 10000.0
NEG_INF = -1e30
LN_EPS = 1e-5
N_EXPERTS = 8
S5_GROUP = 16
S5_STATE = 64
S5_CHUNK = 64
VMEM_LIMIT = 52 * 1024 * 1024
TOKEN_TILE = 512
FF_TILE = 1408
SUBLANES = 8
DMA_START_UNROLL = 8
DMA_WAIT_UNROLL = 32


def _cparams(sem):
    return pltpu.CompilerParams(dimension_semantics=sem, vmem_limit_bytes=VMEM_LIMIT)


def _layer_norm(y, g, b):
    mu = jnp.mean(y, axis=-1, keepdims=True)
    d = y - mu
    var = jnp.mean(d * d, axis=-1, keepdims=True)
    return d * lax.rsqrt(var + LN_EPS) * g + b


def _gelu(x):
    return jax.nn.gelu(x)


def _inproj_kernel(x_ref, w_ref, cos_ref, sin_ref, q_ref, k_ref, v_ref, u_ref, zv_ref):
    xb = x_ref[...].astype(BF16)
    cos = cos_ref[...]
    sin = sin_ref[...]
    lane = lax.broadcasted_iota(jnp.int32, cos.shape, 1)
    first_half = (lane & 32) == 0

    def proj(lo, hi):
        return jnp.dot(xb, w_ref[:, lo:hi], preferred_element_type=F32)

    def rope(blk):
        rot = jnp.where(first_half, pltpu.roll(blk, 96, 1), pltpu.roll(blk, 32, 1))
        return blk * cos + rot * sin

    for j in range(Q_WIDTH // LANES):
        q_ref[:, j * LANES:(j + 1) * LANES] = rope(proj(j * LANES, (j + 1) * LANES)).astype(BF16)
    base = Q_WIDTH
    for j in range(KV_HEADS):
        k_ref[:, j * LANES:(j + 1) * LANES] = rope(proj(base + j * LANES, base + (j + 1) * LANES)).astype(BF16)
    base += KV_DUP_WIDTH
    v_ref[...] = proj(base, base + KV_DUP_WIDTH).astype(BF16)
    base += KV_DUP_WIDTH
    u_ref[...] = _gelu(proj(base, base + GATE_WIDTH)).astype(BF16)
    base += GATE_WIDTH
    zv_ref[...] = _gelu(proj(base, base + GATE_WIDTH)).astype(BF16)


def _inproj(x2d, w_ext, cos_t, sin_t, seq, tm):
    T, D = x2d.shape
    n_ext = w_ext.shape[1]
    per_seq = seq // tm
    row = lambda i: (i, 0)
    return pl.pallas_call(
        _inproj_kernel,
        grid=(T // tm,),
        in_specs=[
            pl.BlockSpec((tm, D), row),
            pl.BlockSpec((D, n_ext), lambda i: (0, 0)),
            pl.BlockSpec((tm, LANES), lambda i: (i % per_seq, 0)),
            pl.BlockSpec((tm, LANES), lambda i: (i % per_seq, 0)),
        ],
        out_specs=[
            pl.BlockSpec((tm, Q_WIDTH), row),
            pl.BlockSpec((tm, KV_DUP_WIDTH), row),
            pl.BlockSpec((tm, KV_DUP_WIDTH), row),
            pl.BlockSpec((tm, GATE_WIDTH), row),
            pl.BlockSpec((tm, GATE_WIDTH), row),
        ],
        out_shape=[
            jax.ShapeDtypeStruct((T, Q_WIDTH), BF16),
            jax.ShapeDtypeStruct((T, KV_DUP_WIDTH), BF16),
            jax.ShapeDtypeStruct((T, KV_DUP_WIDTH), BF16),
            jax.ShapeDtypeStruct((T, GATE_WIDTH), BF16),
            jax.ShapeDtypeStruct((T, GATE_WIDTH), BF16),
        ],
        compiler_params=_cparams(("parallel",)),
        name="inproj_rope_gelu",
    )(x2d, w_ext, cos_t, sin_t)


def _mixer_kernel(sink_ref, q_ref, kp_ref, kc_ref, kn_ref, vp_ref, vc_ref, vn_ref, u_ref, zv_ref,
                  avg_ref, sg_ref, sb_ref, ws_ref, bias_ref, o_ref, kall, vall, *, tiles_per_seq):
    tq = q_ref.shape[0]
    n_blk = tq // WINDOW_BLOCK
    blk = WINDOW_BLOCK
    seq_tile = lax.rem(pl.program_id(0), tiles_per_seq)
    has_prev = seq_tile > 0
    has_next = seq_tile < tiles_per_seq - 1

    kall[0:blk, :] = kp_ref[...]
    kall[blk:blk + tq, :] = kc_ref[...]
    kall[blk + tq:, :] = kn_ref[...]
    vall[0:blk, :] = vp_ref[...]
    vall[blk:blk + tq, :] = vc_ref[...]
    vall[blk + tq:, :] = vn_ref[...]

    qi = lax.broadcasted_iota(jnp.int32, (blk, 3 * blk), 0)
    kj = lax.broadcasted_iota(jnp.int32, (blk, 3 * blk), 1)
    lane = lax.broadcasted_iota(jnp.int32, (blk, LANES), 1)
    low_half = lane < HEAD_DIM
    key_lo_first = jnp.where(has_prev, 0, blk)
    key_hi_last = jnp.where(has_next, 3 * blk, 2 * blk)

    for r in range(n_blk):
        lo = key_lo_first if r == 0 else 0
        hi = key_hi_last if r == n_blk - 1 else 3 * blk
        valid = (kj >= jnp.maximum(qi, lo)) & (kj <= qi + 2 * blk) & (kj < hi)
        rows = slice(r * blk, (r + 1) * blk)
        for h in range(KV_HEADS):
            kd = kall[r * blk:r * blk + 3 * blk, h * LANES:(h + 1) * LANES]
            vd = vall[r * blk:r * blk + 3 * blk, h * LANES:(h + 1) * LANES]
            qa = q_ref[rows, (2 * h) * LANES:(2 * h + 1) * LANES]
            qb = q_ref[rows, (2 * h + 1) * LANES:(2 * h + 2) * LANES]
            zero = jnp.zeros_like(qa)
            lhs = jnp.concatenate([jnp.where(low_half, qa, zero), jnp.where(low_half, zero, qa),
                                   jnp.where(low_half, qb, zero), jnp.where(low_half, zero, qb)], axis=0)
            s = lax.dot_general(lhs, kd, (((1,), (1,)), ((), ())), preferred_element_type=F32)
            s = s * (HEAD_DIM ** -0.5)
            probs = []
            denoms = []
            for g in range(4):
                sink = sink_ref[h * 4 + g]
                sg = jnp.where(valid, s[g * blk:(g + 1) * blk], NEG_INF)
                m = jnp.maximum(jnp.max(sg, axis=-1, keepdims=True), sink)
                p = jnp.exp(sg - m)
                denoms.append(jnp.sum(p, axis=-1, keepdims=True) + jnp.exp(sink - m))
                probs.append(p.astype(BF16))
            pv = jnp.dot(jnp.concatenate(probs, axis=0), vd, preferred_element_type=F32)
            outs = [pv[g * blk:(g + 1) * blk] / denoms[g] for g in range(4)]
            o_ref[rows, (2 * h) * LANES:(2 * h + 1) * LANES] = jnp.where(low_half, outs[0], outs[1]).astype(BF16)
            o_ref[rows, (2 * h + 1) * LANES:(2 * h + 2) * LANES] = jnp.where(low_half, outs[2], outs[3]).astype(BF16)

        vg = zv_ref[rows, :]
        avg = avg_ref[...]
        mean = jnp.dot(vg, avg, preferred_element_type=F32)
        d = vg.astype(F32) - mean
        d2 = d * d
        d2_hi = d2.astype(BF16)
        d2_lo = (d2 - d2_hi.astype(F32)).astype(BF16)
        var = jnp.dot(d2_hi, avg, preferred_element_type=F32) + jnp.dot(d2_lo, avg, preferred_element_type=F32)
        vn = (d * lax.rsqrt(var + LN_EPS) * sg_ref[...] + sb_ref[...]).astype(BF16)
        for m_ in range(GATE_GROUPS // 2):
            cols = slice(m_ * LANES, (m_ + 1) * LANES)
            rhs = vn[:, cols]
            a = jnp.dot(ws_ref[2 * m_], rhs, preferred_element_type=F32)
            b = jnp.dot(ws_ref[2 * m_ + 1], rhs, preferred_element_type=F32)
            sv = jnp.where(low_half, a, b) + bias_ref[:, cols]
            o_ref[rows, Q_WIDTH + m_ * LANES:Q_WIDTH + (m_ + 1) * LANES] = (
                u_ref[rows, cols].astype(F32) * sv).astype(BF16)


def _mixer(q, k, v, u, zv, sink, avg, sgu_g, sgu_b, ws, bias, seq, tq):
    T = q.shape[0]
    r = tq // WINDOW_BLOCK
    tiles_per_seq = seq // tq
    blocks_per_seq = seq // WINDOW_BLOCK

    def prev_map(i, sink_ref):
        first = (i // tiles_per_seq) * blocks_per_seq
        return (jnp.maximum(i * r - 1, first), 0)

    def next_map(i, sink_ref):
        last = (i // tiles_per_seq + 1) * blocks_per_seq - 1
        return (jnp.minimum((i + 1) * r, last), 0)

    row = lambda i, sink_ref: (i, 0)
    const2 = lambda i, sink_ref: (0, 0)
    nb_spec_p = pl.BlockSpec((WINDOW_BLOCK, KV_DUP_WIDTH), prev_map)
    nb_spec_n = pl.BlockSpec((WINDOW_BLOCK, KV_DUP_WIDTH), next_map)
    cur_spec = pl.BlockSpec((tq, KV_DUP_WIDTH), row)
    grid_spec = pltpu.PrefetchScalarGridSpec(
        num_scalar_prefetch=1,
        grid=(T // tq,),
        in_specs=[
            pl.BlockSpec((tq, Q_WIDTH), row),
            nb_spec_p, cur_spec, nb_spec_n,
            nb_spec_p, cur_spec, nb_spec_n,
            pl.BlockSpec((tq, GATE_WIDTH), row),
            pl.BlockSpec((tq, GATE_WIDTH), row),
            pl.BlockSpec((GATE_WIDTH, GATE_WIDTH), const2),
            pl.BlockSpec((1, GATE_WIDTH), const2),
            pl.BlockSpec((1, GATE_WIDTH), const2),
            pl.BlockSpec((GATE_GROUPS, WINDOW_BLOCK, WINDOW_BLOCK), lambda i, sink_ref: (0, 0, 0)),
            pl.BlockSpec((WINDOW_BLOCK, GATE_WIDTH), const2),
        ],
        out_specs=pl.BlockSpec((tq, Q_WIDTH + GATE_WIDTH), row),
        scratch_shapes=[
            pltpu.VMEM((tq + 2 * WINDOW_BLOCK, KV_DUP_WIDTH), BF16),
            pltpu.VMEM((tq + 2 * WINDOW_BLOCK, KV_DUP_WIDTH), BF16),
        ],
    )
    return pl.pallas_call(
        functools.partial(_mixer_kernel, tiles_per_seq=tiles_per_seq),
        grid_spec=grid_spec,
        out_shape=jax.ShapeDtypeStruct((T, Q_WIDTH + GATE_WIDTH), BF16),
        compiler_params=_cparams(("parallel",)),
        name="window_attn_spatial_gate",
    )(sink, q, k, k, k, v, v, v, u, zv, avg, sgu_g, sgu_b, ws, bias)


def _proj_ln_kernel(a_ref, x_ref, w_ref, g_ref, b_ref, o_ref, *, alpha):
    y = jnp.dot(a_ref[...], w_ref[...], preferred_element_type=F32)
    o_ref[...] = _layer_norm(alpha * x_ref[...] + y, g_ref[...], b_ref[...])


def _proj_ln(a, x2d, w, ln_g, ln_b, alpha):
    T, D = x2d.shape
    K = a.shape[1]
    tm = TOKEN_TILE
    row = lambda i: (i, 0)
    const = lambda i: (0, 0)
    return pl.pallas_call(
        functools.partial(_proj_ln_kernel, alpha=alpha),
        grid=(T // tm,),
        in_specs=[pl.BlockSpec((tm, K), row), pl.BlockSpec((tm, D), row), pl.BlockSpec((K, D), const),
                  pl.BlockSpec((1, D), const), pl.BlockSpec((1, D), const)],
        out_specs=pl.BlockSpec((tm, D), row),
        out_shape=jax.ShapeDtypeStruct((T, D), F32),
        compiler_params=_cparams(("parallel",)),
        name="proj_residual_ln",
    )(a, x2d, w, ln_g, ln_b)


def _swiglu_ln_kernel(x_ref, wg_ref, wu_ref, wd_ref, g_ref, b_ref, o_ref, xb, acc, *, alpha):
    f = pl.program_id(1)

    @pl.when(f == 0)
    def _():
        xb[...] = x_ref[...].astype(BF16)

    xv = xb[...]
    gate = jnp.dot(xv, wg_ref[...], preferred_element_type=F32)
    up = jnp.dot(xv, wu_ref[...], preferred_element_type=F32)
    h = (jax.nn.silu(gate) * up).astype(BF16)
    part = jnp.dot(h, wd_ref[...], preferred_element_type=F32)

    @pl.when(f == 0)
    def _():
        acc[...] = part

    @pl.when(f > 0)
    def _():
        acc[...] += part

    @pl.when(f == pl.num_programs(1) - 1)
    def _():
        o_ref[...] = _layer_norm(alpha * x_ref[...] + acc[...], g_ref[...], b_ref[...])


def _swiglu_ln(x2d, wg, wu, wd, ln_g, ln_b, alpha, tm, tf):
    T, D = x2d.shape
    FF = wg.shape[1]
    return pl.pallas_call(
        functools.partial(_swiglu_ln_kernel, alpha=alpha),
        grid=(T // tm, FF // tf),
        in_specs=[
            pl.BlockSpec((tm, D), lambda i, f: (i, 0)),
            pl.BlockSpec((D, tf), lambda i, f: (0, f)),
            pl.BlockSpec((D, tf), lambda i, f: (0, f)),
            pl.BlockSpec((tf, D), lambda i, f: (f, 0)),
            pl.BlockSpec((1, D), lambda i, f: (0, 0)),
            pl.BlockSpec((1, D), lambda i, f: (0, 0)),
        ],
        out_specs=pl.BlockSpec((tm, D), lambda i, f: (i, 0)),
        out_shape=jax.ShapeDtypeStruct((T, D), F32),
        scratch_shapes=[pltpu.VMEM((tm, D), BF16), pltpu.VMEM((tm, D), F32)],
        compiler_params=_cparams(("parallel", "arbitrary")),
        name="swiglu_residual_ln",
    )(x2d, wg, wu, wd, ln_g, ln_b)


def _lane_block_transpose(v, lane):
    for k in range(3):
        m = 1 << k
        high = ((lane >> (4 + k)) & 1) == 1
        nxt = list(v)
        for i in range(SUBLANES):
            if i & m == 0:
                a, b = v[i], v[i + m]
                nxt[i] = jnp.where(high, pltpu.roll(b, S5_GROUP * m, 1), a)
                nxt[i + m] = jnp.where(high, b, pltpu.roll(a, LANES - S5_GROUP * m, 1))
        v = nxt
    return v


def _glu_chunks_ln_kernel(y_ref, x_ref, d_ref, wv_ref, wg_ref, g_ref, b_ref, o_ref, ybuf, *, alpha):
    lane = lax.broadcasted_iota(jnp.int32, (SUBLANES, LANES), 1)
    n_seq, L, D = x_ref.shape
    for o in range(D // LANES):
        cols = slice(o * LANES, (o + 1) * LANES)
        for th in range(L // SUBLANES):
            z = [y_ref[o * SUBLANES + s, :, th * LANES:(th + 1) * LANES] for s in range(SUBLANES)]
            v = _lane_block_transpose(z, lane)
            for tl in range(SUBLANES):
                ybuf[:, th * SUBLANES + tl, cols] = v[tl]
    x = x_ref[...].reshape(n_seq * L, D)
    a = _gelu(ybuf[...].reshape(n_seq * L, D) + d_ref[...] * x).astype(BF16)
    val = jnp.dot(a, wv_ref[...], preferred_element_type=F32)
    gate = jnp.dot(a, wg_ref[...], preferred_element_type=F32)
    out = _layer_norm(alpha * x + val * jax.nn.sigmoid(gate), g_ref[...], b_ref[...])
    o_ref[...] = out.reshape(n_seq, L, D)


def _glu_chunks_ln(y4, x3d, d_skip, wv, wg, ln_g, ln_b, alpha):
    Bsz, S, D = x3d.shape
    G, n, _, W = y4.shape
    L = S5_CHUNK
    tile = lambda b, j: (b, j, 0)
    const = lambda b, j: (0, 0)
    return pl.pallas_call(
        functools.partial(_glu_chunks_ln_kernel, alpha=alpha),
        grid=(Bsz // SUBLANES, n),
        in_specs=[pl.BlockSpec((G, None, SUBLANES, W), lambda b, j: (0, j, b, 0)),
                  pl.BlockSpec((SUBLANES, L, D), tile),
                  pl.BlockSpec((1, D), const),
                  pl.BlockSpec((D, D), const), pl.BlockSpec((D, D), const),
                  pl.BlockSpec((1, D), const), pl.BlockSpec((1, D), const)],
        out_specs=pl.BlockSpec((SUBLANES, L, D), tile),
        out_shape=jax.ShapeDtypeStruct((Bsz, S, D), F32),
        scratch_shapes=[pltpu.VMEM((SUBLANES, L, D), F32)],
        compiler_params=_cparams(("parallel", "parallel")),
        name="glu_residual_ln",
    )(y4, x3d, d_skip.reshape(1, D), wv, wg, ln_g, ln_b)


def _s5_kernel(u_ref, wt_ref, ws_ref, wo_ref, are_ref, aim_ref, o_ref, gbuf, sbuf, *, n_chunks, bsz):
    P2 = 2 * S5_STATE
    ub = u_ref[...]
    gbuf[...] = jnp.dot(ub, ws_ref[...], preferred_element_type=F32)
    a_re = are_ref[...]
    a_im = aim_ref[...]
    lane = lax.broadcasted_iota(jnp.int32, (bsz, P2), 1)
    fwd = lane < S5_STATE
    s_re = jnp.zeros((bsz, P2), F32)
    s_im = jnp.zeros((bsz, P2), F32)
    for i in range(n_chunks):
        if i > 0:
            rf = slice((i - 1) * bsz, i * bsz)
            rb = slice((n_chunks - i) * bsz, (n_chunks - i + 1) * bsz)
            g_re = jnp.where(fwd, gbuf[rf, 0:P2], gbuf[rb, 0:P2])
            g_im = jnp.where(fwd, gbuf[rf, P2:2 * P2], gbuf[rb, P2:2 * P2])
            s_re, s_im = (a_re * s_re - a_im * s_im + g_re, a_re * s_im + a_im * s_re + g_im)
        of = slice(i * bsz, (i + 1) * bsz)
        ob = slice((n_chunks - 1 - i) * bsz, (n_chunks - i) * bsz)
        sbuf[of, 0:S5_STATE] = s_re[:, 0:S5_STATE]
        sbuf[ob, S5_STATE:P2] = s_re[:, S5_STATE:P2]
        sbuf[of, P2:P2 + S5_STATE] = s_im[:, 0:S5_STATE]
        sbuf[ob, P2 + S5_STATE:2 * P2] = s_im[:, S5_STATE:P2]
    y = jnp.dot(ub, wt_ref[...], preferred_element_type=F32)
    o_ref[...] = y + jnp.dot(sbuf[...].astype(BF16), wo_ref[...], preferred_element_type=F32)


def _s5_tables(lam_re, lam_im, log_dt, b_re, b_im, c_re, c_im):
    L = S5_CHUNK
    hp = lax.Precision.HIGHEST
    G, P = lam_re.shape[1], lam_re.shape[2]
    C = b_re.shape[-1]
    dt = jnp.exp(log_dt)[..., None]
    zr, zi = lam_re * dt, lam_im * dt
    er = jnp.exp(zr)
    lbr, lbi = er * jnp.cos(zi), er * jnp.sin(zi)
    nr, ni = lbr - 1.0, lbi
    den = lam_re * lam_re + lam_im * lam_im
    cr = (nr * lam_re + ni * lam_im) / den
    ci = (ni * lam_re - nr * lam_im) / den
    bbr = cr[..., None] * b_re - ci[..., None] * b_im
    bbi = cr[..., None] * b_im + ci[..., None] * b_re
    k = jnp.arange(L + 1, dtype=F32)[:, None, None, None]
    mag = jnp.exp(k * zr[None])
    pwr, pwi = mag * jnp.cos(k * zi[None]), mag * jnp.sin(k * zi[None])

    cbr = c_re[..., None] * bbr[:, :, None] - c_im[..., None] * bbi[:, :, None]
    cbi = c_re[..., None] * bbi[:, :, None] + c_im[..., None] * bbr[:, :, None]
    kern = (jnp.einsum('kzgp,zgcpd->zgkcd', pwr[:L], cbr, precision=hp)
            - jnp.einsum('kzgp,zgcpd->zgkcd', pwi[:L], cbi, precision=hp))
    kf, kb = kern[0], kern[1]
    kfull = jnp.concatenate([kb[:, 1:][:, ::-1], (kf[:, 0] + kb[:, 0])[:, None], kf[:, 1:]], axis=1)
    by_in = kfull.transpose(0, 3, 1, 2).astype(BF16)
    window = lambda start: lax.dynamic_slice_in_dim(by_in, start, L, axis=2)
    w_toep = jax.vmap(window, out_axes=1)(L - 1 - jnp.arange(L)).reshape(G, L * C, L * C)

    pf_r, pf_i = pwr[:L, 0][::-1], pwi[:L, 0][::-1]
    pb_r, pb_i = pwr[:L, 1], pwi[:L, 1]

    def state_cols(pr, pi, br, bi):
        re = pr[..., None] * br[None] - pi[..., None] * bi[None]
        im = pr[..., None] * bi[None] + pi[..., None] * br[None]
        return re.transpose(1, 0, 3, 2), im.transpose(1, 0, 3, 2)

    sf_r, sf_i = state_cols(pf_r, pf_i, bbr[0], bbi[0])
    sb_r, sb_i = state_cols(pb_r, pb_i, bbr[1], bbi[1])
    w_state = jnp.concatenate([sf_r, sb_r, sf_i, sb_i], axis=-1).reshape(G, L * C, 4 * P)

    def out_rows(pr, pi, cre, cim):
        er_ = cre[None] * pr[:, :, None] - cim[None] * pi[:, :, None]
        ei_ = cre[None] * pi[:, :, None] + cim[None] * pr[:, :, None]
        return er_.transpose(1, 3, 0, 2), -ei_.transpose(1, 3, 0, 2)

    of_r, of_i = out_rows(pwr[1:L + 1, 0], pwi[1:L + 1, 0], c_re[0], c_im[0])
    ob_r, ob_i = out_rows(pwr[1:L + 1, 1][::-1], pwi[1:L + 1, 1][::-1], c_re[1], c_im[1])
    w_out = jnp.concatenate([of_r, ob_r, of_i, ob_i], axis=1).reshape(G, 4 * P, L * C)

    a_re = jnp.concatenate([pwr[L, 0], pwr[L, 1]], axis=-1)[:, None, :]
    a_im = jnp.concatenate([pwi[L, 0], pwi[L, 1]], axis=-1)[:, None, :]
    return w_toep, w_state.astype(BF16), w_out.astype(BF16), a_re, a_im


def _s5(x3d, tables):
    w_toep, w_state, w_out, a_re, a_im = tables
    Bsz, S, D = x3d.shape
    L, C = S5_CHUNK, S5_GROUP
    G = D // C
    n = S // L
    R = n * Bsz
    W = L * C
    u = x3d.astype(BF16).reshape(Bsz, n, L, G, C).transpose(3, 1, 0, 2, 4).reshape(G, R, W)
    grp = lambda g: (g, 0, 0)
    y = pl.pallas_call(
        functools.partial(_s5_kernel, n_chunks=n, bsz=Bsz),
        grid=(G,),
        in_specs=[
            pl.BlockSpec((None, R, W), grp),
            pl.BlockSpec((None, W, W), grp),
            pl.BlockSpec((None, W, 4 * S5_STATE), grp),
            pl.BlockSpec((None, 4 * S5_STATE, W), grp),
            pl.BlockSpec((None, 1, 2 * S5_STATE), grp),
            pl.BlockSpec((None, 1, 2 * S5_STATE), grp),
        ],
        out_specs=pl.BlockSpec((None, R, W), grp),
        out_shape=jax.ShapeDtypeStruct((G, R, W), F32),
        scratch_shapes=[pltpu.VMEM((R, 4 * S5_STATE), F32), pltpu.VMEM((R, 4 * S5_STATE), F32)],
        compiler_params=_cparams(("parallel",)),
        name="s5_chunked_conv",
    )(u, w_toep, w_state, w_out, a_re, a_im)
    return y.reshape(G, n, Bsz, W)


def _router_kernel(x_ref, wh_ref, wl_ref, o_ref):
    x = x_ref[...]
    xh = x.astype(BF16)
    xl = (x - xh.astype(F32)).astype(BF16)
    wh = wh_ref[...]
    logits = (jnp.dot(xh, wh, preferred_element_type=F32) + jnp.dot(xl, wh, preferred_element_type=F32)
              + jnp.dot(xh, wl_ref[...], preferred_element_type=F32))
    lane = lax.broadcasted_iota(jnp.int32, logits.shape, 1)
    logits = jnp.where(lane < N_EXPERTS, logits, -jnp.inf)
    v1 = jnp.max(logits, axis=-1, keepdims=True)
    lane_f = lane.astype(F32)
    i1 = jnp.min(jnp.where(logits == v1, lane_f, float(LANES)), axis=-1, keepdims=True)
    rest = jnp.where(lane_f == i1, -jnp.inf, logits)
    v2 = jnp.max(rest, axis=-1, keepdims=True)
    i2 = jnp.min(jnp.where(rest == v2, lane_f, float(LANES)), axis=-1, keepdims=True)
    e2 = jnp.exp(v2 - v1)
    den = 1.0 + e2
    w1 = 1.0 / den
    w2 = e2 / den
    o_ref[...] = jnp.where(lane == 0, w1, jnp.where(lane == 1, w2, jnp.where(
        lane == 2, i1, jnp.where(lane == 3, i2, 0.0))))


def _router(x2d, router_w, tm):
    T, D = x2d.shape
    wpad = jnp.zeros((D, LANES), F32).at[:, :N_EXPERTS].set(router_w)
    wh = wpad.astype(BF16)
    wl = (wpad - wh.astype(F32)).astype(BF16)
    return pl.pallas_call(
        _router_kernel,
        grid=(T // tm,),
        in_specs=[pl.BlockSpec((tm, D), lambda i: (i, 0)),
                  pl.BlockSpec((D, LANES), lambda i: (0, 0)),
                  pl.BlockSpec((D, LANES), lambda i: (0, 0))],
        out_specs=pl.BlockSpec((tm, LANES), lambda i: (i, 0)),
        out_shape=jax.ShapeDtypeStruct((T, LANES), F32),
        compiler_params=_cparams(("parallel",)),
        name="router_top2",
    )(x2d, wh, wl)


def _route_plan(info, tm):
    T = info.shape[0]
    n_pairs = 2 * T
    n_tiles = n_pairs // tm + N_EXPERTS
    n_slots = n_tiles * tm
    gate_w = info[:, 0:2].T.reshape(-1)
    expert = info[:, 2:4].astype(jnp.int32).T.reshape(-1)
    order = jnp.sort(expert * n_pairs + jnp.arange(n_pairs, dtype=jnp.int32)) % n_pairs
    counts = jnp.sum(expert[:, None] == jnp.arange(N_EXPERTS, dtype=jnp.int32)[None, :], axis=0, dtype=jnp.int32)
    padded = ((counts + tm - 1) // tm) * tm
    pad_end = jnp.cumsum(padded)
    pad_start = pad_end - padded
    cmp_start = jnp.cumsum(counts) - counts
    tile_start = jnp.arange(n_tiles, dtype=jnp.int32) * tm
    tile_expert = jnp.minimum(jnp.sum(tile_start[:, None] >= pad_end[None, :], axis=1, dtype=jnp.int32),
                              N_EXPERTS - 1)
    n_valid = jnp.clip(counts[tile_expert] - (tile_start - pad_start[tile_expert]), 0, tm)
    n_valid = jnp.where(tile_start < pad_end[-1], n_valid, 0).astype(jnp.int32)
    slot = jnp.arange(n_slots, dtype=jnp.int32)
    s_exp = jnp.repeat(tile_expert, tm)
    rank = slot - pad_start[s_exp]
    valid = (rank < counts[s_exp]) & (slot < pad_end[-1])
    pair = order[jnp.clip(cmp_start[s_exp] + rank, 0, n_pairs - 1)]
    slot_tok = jnp.where(valid, pair % T, 0)
    slot_dst = jnp.where(valid, pair, 0)
    slot_w = jnp.where(valid, gate_w[pair], 0.0)
    return (tile_expert, n_valid, slot_tok.reshape(n_tiles, 1, tm), slot_dst.reshape(n_tiles, 1, tm),
            slot_w.reshape(n_slots, 1), n_tiles)


def _moe_kernel(texp_ref, nvalid_ref, tok0_ref, tokn_ref, dst_ref, x_hbm, sw_ref, wg_ref, wu_ref, wd_ref,
                out_hbm, xbuf, xb, acc, obuf, sem_in, sem_out):
    i = pl.program_id(0)
    f = pl.program_id(1)
    n_f = pl.num_programs(1)
    n_tiles = pl.num_programs(0)
    tm = xb.shape[0]
    rows_here = nvalid_ref[i]
    rows_prev = nvalid_ref[jnp.maximum(i - 1, 0)]
    active = rows_here > 0
    next_active = jnp.logical_and(i + 1 < n_tiles, nvalid_ref[jnp.minimum(i + 1, n_tiles - 1)] > 0)
    slot = lax.rem(i, 2)

    def gather_copy(tok_ref, r, buf_slot):
        return pltpu.make_async_copy(x_hbm.at[pl.ds(tok_ref[0, 0, r], 1), :],
                                     xbuf.at[buf_slot, pl.ds(r, 1), :], sem_in.at[buf_slot])

    def scatter_copy(r):
        return pltpu.make_async_copy(obuf.at[pl.ds(r, 1), :],
                                     out_hbm.at[pl.ds(dst_ref[0, 0, r], 1), :], sem_out.at[0])

    def row_wait_in(buf_slot):
        pltpu.make_async_copy(x_hbm.at[pl.ds(0, 1), :], xbuf.at[buf_slot, pl.ds(0, 1), :],
                              sem_in.at[buf_slot]).wait()

    def row_wait_out():
        pltpu.make_async_copy(obuf.at[pl.ds(0, 1), :], out_hbm.at[pl.ds(0, 1), :], sem_out.at[0]).wait()

    def blocked_loop(n_rows, unroll, fn):
        n_blocks = lax.shift_right_logical(n_rows, unroll.bit_length() - 1)

        def block(k, c):
            for j in range(unroll):
                fn(k * unroll + j)
            return c
        lax.fori_loop(0, n_blocks, block, 0)

        def single(r, c):
            fn(r)
            return c
        lax.fori_loop(n_blocks * unroll, n_rows, single, 0)

    def start_gather(tok_ref, buf_slot):
        blocked_loop(jnp.int32(tm), DMA_START_UNROLL, lambda r: gather_copy(tok_ref, r, buf_slot).start())

    def wait_gather(buf_slot):
        blocked_loop(jnp.int32(tm), DMA_WAIT_UNROLL, lambda r: row_wait_in(buf_slot))

    def wait_scatter(n_rows):
        blocked_loop(n_rows, DMA_WAIT_UNROLL, lambda r: row_wait_out())

    @pl.when(jnp.logical_and(f == 0, i == 0))
    def _():
        start_gather(tok0_ref, 0)

    @pl.when(jnp.logical_and(f == 0, active))
    def _():
        wait_gather(slot)
        xb[...] = xbuf[slot].astype(BF16)

        @pl.when(next_active)
        def _():
            start_gather(tokn_ref, 1 - slot)

    @pl.when(active)
    def _():
        xv = xb[...]
        gate = jnp.dot(xv, wg_ref[...], preferred_element_type=F32)
        up = jnp.dot(xv, wu_ref[...], preferred_element_type=F32)
        h = (jax.nn.silu(gate) * up).astype(BF16)
        part = jnp.dot(h, wd_ref[...], preferred_element_type=F32)

        @pl.when(f == 0)
        def _():
            acc[...] = part

        @pl.when(f > 0)
        def _():
            acc[...] += part

    @pl.when(jnp.logical_and(f == n_f - 1, active))
    def _():
        @pl.when(i > 0)
        def _():
            wait_scatter(rows_prev)

        obuf[...] = acc[...] * sw_ref[...]

        blocked_loop(rows_here, DMA_START_UNROLL, lambda r: scatter_copy(r).start())

        @pl.when(jnp.logical_not(next_active))
        def _():
            wait_scatter(rows_here)


def _moe_experts(x2d, plan, wg, wu, wd, tm, tf):
    tile_expert, n_valid, slot_tok, slot_dst, slot_w, n_tiles = plan
    T, D = x2d.shape
    FF = wg.shape[2]
    smem_blk = lambda imap: pl.BlockSpec((1, 1, tm), imap, memory_space=pltpu.SMEM)
    grid_spec = pltpu.PrefetchScalarGridSpec(
        num_scalar_prefetch=2,
        grid=(n_tiles, FF // tf),
        in_specs=[
            smem_blk(lambda i, f, te, nu: (0, 0, 0)),
            smem_blk(lambda i, f, te, nu: (jnp.minimum(i + 1, n_tiles - 1), 0, 0)),
            smem_blk(lambda i, f, te, nu: (i, 0, 0)),
            pl.BlockSpec(memory_space=pl.ANY),
            pl.BlockSpec((tm, 1), lambda i, f, te, nu: (i, 0)),
            pl.BlockSpec((None, D, tf), lambda i, f, te, nu: (te[i], 0, f)),
            pl.BlockSpec((None, D, tf), lambda i, f, te, nu: (te[i], 0, f)),
            pl.BlockSpec((None, tf, D), lambda i, f, te, nu: (te[i], f, 0)),
        ],
        out_specs=pl.BlockSpec(memory_space=pl.ANY),
        scratch_shapes=[
            pltpu.VMEM((2, tm, D), F32),
            pltpu.VMEM((tm, D), BF16),
            pltpu.VMEM((tm, D), F32),
            pltpu.VMEM((tm, D), F32),
            pltpu.SemaphoreType.DMA((2,)),
            pltpu.SemaphoreType.DMA((1,)),
        ],
    )
    return pl.pallas_call(
        _moe_kernel,
        grid_spec=grid_spec,
        out_shape=jax.ShapeDtypeStruct((2 * T, D), F32),
        compiler_params=_cparams(("arbitrary", "arbitrary")),
        name="moe_grouped_swiglu",
    )(tile_expert, n_valid, slot_tok, slot_tok, slot_dst, x2d, slot_w, wg, wu, wd)


def _combine_ln_kernel(x_ref, y0_ref, y1_ref, g_ref, b_ref, o_ref, *, alpha):
    o_ref[...] = _layer_norm(alpha * x_ref[...] + (y0_ref[...] + y1_ref[...]), g_ref[...], b_ref[...])


def _combine_ln(x2d, y, ln_g, ln_b, alpha, tm):
    T, D = x2d.shape
    nt = T // tm
    return pl.pallas_call(
        functools.partial(_combine_ln_kernel, alpha=alpha),
        grid=(nt,),
        in_specs=[pl.BlockSpec((tm, D), lambda i: (i, 0)),
                  pl.BlockSpec((tm, D), lambda i: (i, 0)),
                  pl.BlockSpec((tm, D), lambda i: (i + nt, 0)),
                  pl.BlockSpec((1, D), lambda i: (0, 0)),
                  pl.BlockSpec((1, D), lambda i: (0, 0))],
        out_specs=pl.BlockSpec((tm, D), lambda i: (i, 0)),
        out_shape=jax.ShapeDtypeStruct((T, D), F32),
        compiler_params=_cparams(("parallel",)),
        name="moe_combine_ln",
    )(x2d, y, y, ln_g, ln_b)


def _rope_tables(seq):
    pos = jnp.arange(seq, dtype=F32)
    inv_freq = ROPE_THETA ** (-jnp.arange(0, HEAD_DIM, 2, dtype=F32) / HEAD_DIM)
    ang = pos[:, None] * inv_freq[None, :]
    cos, sin = jnp.cos(ang), jnp.sin(ang)
    return jnp.tile(cos, (1, 4)), jnp.tile(jnp.concatenate([-sin, sin], axis=1), (1, 2))


def kernel(x, ln_g, ln_b, mix_w_in, attn_sink, sgu_w, sgu_b, sgu_ln_g, sgu_ln_b, mix_w_out, ffn_w_gate, ffn_w_up, ffn_w_down, s5_lam_re, s5_lam_im, s5_log_dt, s5_b_re, s5_b_im, s5_c_re, s5_c_im, s5_d, glu_w_val, glu_w_gate, router_w, moe_w_gate, moe_w_up, moe_w_down):
    Bsz, S, D = x.shape
    depth = ln_g.shape[0]
    T = Bsz * S
    alpha = (2.0 * depth) ** 0.25
    tm = TOKEN_TILE
    cos_t, sin_t = _rope_tables(S)
    avg = jnp.kron(jnp.eye(GATE_GROUPS, dtype=F32), jnp.full((HEAD_DIM, HEAD_DIM), 1.0 / HEAD_DIM, F32)).astype(BF16)
    x2d = x.reshape(T, D)
    for layer in range(depth):
        i = layer // 2
        g0, b0 = ln_g[layer, 0][None, :], ln_b[layer, 0][None, :]
        g1, b1 = ln_g[layer, 1][None, :], ln_b[layer, 1][None, :]
        if layer % 2 == 0:
            w = mix_w_in[i]
            wq, wk, wv, wz = (w[:, :Q_WIDTH], w[:, Q_WIDTH:Q_WIDTH + 128], w[:, Q_WIDTH + 128:Q_WIDTH + 256],
                              w[:, Q_WIDTH + 256:])
            dup = lambda m: jnp.concatenate([m[:, :HEAD_DIM], m[:, :HEAD_DIM], m[:, HEAD_DIM:], m[:, HEAD_DIM:]], axis=1)
            w_ext = jnp.concatenate([wq, dup(wk), dup(wv), wz], axis=1).astype(BF16)
            q, k, v, u, zv = _inproj(x2d, w_ext, cos_t, sin_t, S, tm)
            bias = jnp.repeat(sgu_b[i].T, HEAD_DIM, axis=1)
            mixed = _mixer(q, k, v, u, zv, attn_sink[i], avg, sgu_ln_g[i].reshape(1, -1),
                           sgu_ln_b[i].reshape(1, -1), sgu_w[i].astype(BF16), bias, S, tm)
            x2d = _proj_ln(mixed, x2d, mix_w_out[i].astype(BF16), g0, b0, alpha)
            x2d = _swiglu_ln(x2d, ffn_w_gate[i].astype(BF16), ffn_w_up[i].astype(BF16),
                             ffn_w_down[i].astype(BF16), g1, b1, alpha, tm, FF_TILE)
        else:
            tables = _s5_tables(s5_lam_re[i], s5_lam_im[i], s5_log_dt[i], s5_b_re[i], s5_b_im[i],
                                s5_c_re[i], s5_c_im[i])
            y = _s5(x2d.reshape(Bsz, S, D), tables)
            x2d = _glu_chunks_ln(y, x2d.reshape(Bsz, S, D), s5_d[i], glu_w_val[i].astype(BF16),
                                 glu_w_gate[i].astype(BF16), g0, b0, alpha).reshape(T, D)
            info = _router(x2d, router_w[i], tm)
            plan = _route_plan(info, tm)
            y = _moe_experts(x2d, plan, moe_w_gate[i].astype(BF16), moe_w_up[i].astype(BF16),
                             moe_w_down[i].astype(BF16), tm, FF_TILE)
            x2d = _combine_ln(x2d, y, g1, b1, alpha, tm)
    return x2d.reshape(Bsz, S, D)
```

```python
import functools
import math

import jax
import jax.numpy as jnp
from jax import lax
from jax.experimental import pallas as pl
from jax.experimental.pallas import tpu as pltpu

F32 = jnp.float32
BF16 = jnp.bfloat16

LANES = 128
HEAD_DIM = 64
Q_HEADS = 8
KV_HEADS = 2
Q_WIDTH = Q_HEADS * HEAD_DIM
KV_DUP_WIDTH = KV_HEADS * LANES
GATE_WIDTH = 512
GATE_GROUPS = 8
WINDOW_BLOCK = 128
ROPE_THETA = 10000.0
NEG_INF = -1e30
LN_EPS = 1e-5
N_EXPERTS = 8
S5_GROUP = 16
S5_STATE = 64
S5_CHUNK = 64
VMEM_LIMIT = 52 * 1024 * 1024
TOKEN_TILE = 512
FF_TILE = 1408
SUBLANES = 8
DMA_START_UNROLL = 8
DMA_WAIT_UNROLL = 32


def _cparams(sem):
    return pltpu.CompilerParams(dimension_semantics=sem, vmem_limit_bytes=VMEM_LIMIT)


def _layer_norm(y, g, b):
    mu = jnp.mean(y, axis=-1, keepdims=True)
    d = y - mu
    var = jnp.mean(d * d, axis=-1, keepdims=True)
    return d * lax.rsqrt(var + LN_EPS) * g + b


def _gelu(x):
    return jax.nn.gelu(x)


def _inproj_kernel(x_ref, w_ref, cos_ref, sin_ref, q_ref, k_ref, v_ref, u_ref, zv_ref):
    xb = x_ref[...].astype(BF16)
    cos = cos_ref[...]
    sin = sin_ref[...]
    lane = lax.broadcasted_iota(jnp.int32, cos.shape, 1)
    first_half = (lane & 32) == 0

    def proj(lo, hi):
        return jnp.dot(xb, w_ref[:, lo:hi], preferred_element_type=F32)

    def rope(blk):
        rot = jnp.where(first_half, pltpu.roll(blk, 96, 1), pltpu.roll(blk, 32, 1))
        return blk * cos + rot * sin

    for j in range(Q_WIDTH // LANES):
        q_ref[:, j * LANES:(j + 1) * LANES] = rope(proj(j * LANES, (j + 1) * LANES)).astype(BF16)
    base = Q_WIDTH
    for j in range(KV_HEADS):
        k_ref[:, j * LANES:(j + 1) * LANES] = rope(proj(base + j * LANES, base + (j + 1) * LANES)).astype(BF16)
    base += KV_DUP_WIDTH
    v_ref[...] = proj(base, base + KV_DUP_WIDTH).astype(BF16)
    base += KV_DUP_WIDTH
    u_ref[...] = _gelu(proj(base, base + GATE_WIDTH)).astype(BF16)
    base += GATE_WIDTH
    zv_ref[...] = _gelu(proj(base, base + GATE_WIDTH)).astype(BF16)


def _inproj(x2d, w_ext, cos_t, sin_t, seq, tm):
    T, D = x2d.shape
    n_ext = w_ext.shape[1]
    per_seq = seq // tm
    row = lambda i: (i, 0)
    return pl.pallas_call(
        _inproj_kernel,
        grid=(T // tm,),
        in_specs=[
            pl.BlockSpec((tm, D), row),
            pl.BlockSpec((D, n_ext), lambda i: (0, 0)),
            pl.BlockSpec((tm, LANES), lambda i: (i % per_seq, 0)),
            pl.BlockSpec((tm, LANES), lambda i: (i % per_seq, 0)),
        ],
        out_specs=[
            pl.BlockSpec((tm, Q_WIDTH), row),
            pl.BlockSpec((tm, KV_DUP_WIDTH), row),
            pl.BlockSpec((tm, KV_DUP_WIDTH), row),
            pl.BlockSpec((tm, GATE_WIDTH), row),
            pl.BlockSpec((tm, GATE_WIDTH), row),
        ],
        out_shape=[
            jax.ShapeDtypeStruct((T, Q_WIDTH), BF16),
            jax.ShapeDtypeStruct((T, KV_DUP_WIDTH), BF16),
            jax.ShapeDtypeStruct((T, KV_DUP_WIDTH), BF16),
            jax.ShapeDtypeStruct((T, GATE_WIDTH), BF16),
            jax.ShapeDtypeStruct((T, GATE_WIDTH), BF16),
        ],
        compiler_params=_cparams(("parallel",)),
        name="inproj_rope_gelu",
    )(x2d, w_ext, cos_t, sin_t)


def _mixer_kernel(sink_ref, q_ref, kp_ref, kc_ref, kn_ref, vp_ref, vc_ref, vn_ref, u_ref, zv_ref,
                  avg_ref, sg_ref, sb_ref, ws_ref, bias_ref, o_ref, kall, vall, *, tiles_per_seq):
    tq = q_ref.shape[0]
    n_blk = tq // WINDOW_BLOCK
    blk = WINDOW_BLOCK
    seq_tile = lax.rem(pl.program_id(0), tiles_per_seq)
    has_prev = seq_tile > 0
    has_next = seq_tile < tiles_per_seq - 1

    kall[0:blk, :] = kp_ref[...]
    kall[blk:blk + tq, :] = kc_ref[...]
    kall[blk + tq:, :] = kn_ref[...]
    vall[0:blk, :] = vp_ref[...]
    vall[blk:blk + tq, :] = vc_ref[...]
    vall[blk + tq:, :] = vn_ref[...]

    qi = lax.broadcasted_iota(jnp.int32, (blk, 3 * blk), 0)
    kj = lax.broadcasted_iota(jnp.int32, (blk, 3 * blk), 1)
    lane = lax.broadcasted_iota(jnp.int32, (blk, LANES), 1)
    low_half = lane < HEAD_DIM
    key_lo_first = jnp.where(has_prev, 0, blk)
    key_hi_last = jnp.where(has_next, 3 * blk, 2 * blk)

    for r in range(n_blk):
        lo = key_lo_first if r == 0 else 0
        hi = key_hi_last if r == n_blk - 1 else 3 * blk
        valid = (kj >= jnp.maximum(qi, lo)) & (kj <= qi + 2 * blk) & (kj < hi)
        rows = slice(r * blk, (r + 1) * blk)
        for h in range(KV_HEADS):
            kd = kall[r * blk:r * blk + 3 * blk, h * LANES:(h + 1) * LANES]
            vd = vall[r * blk:r * blk + 3 * blk, h * LANES:(h + 1) * LANES]
            qa = q_ref[rows, (2 * h) * LANES:(2 * h + 1) * LANES]
            qb = q_ref[rows, (2 * h + 1) * LANES:(2 * h + 2) * LANES]
            zero = jnp.zeros_like(qa)
            lhs = jnp.concatenate([jnp.where(low_half, qa, zero), jnp.where(low_half, zero, qa),
                                   jnp.where(low_half, qb, zero), jnp.where(low_half, zero, qb)], axis=0)
            s = lax.dot_general(lhs, kd, (((1,), (1,)), ((), ())), preferred_element_type=F32)
            s = s * (HEAD_DIM ** -0.5)
            probs = []
            denoms = []
            for g in range(4):
                sink = sink_ref[h * 4 + g]
                sg = jnp.where(valid, s[g * blk:(g + 1) * blk], NEG_INF)
                m = jnp.maximum(jnp.max(sg, axis=-1, keepdims=True), sink)
                p = jnp.exp(sg - m)
                denoms.append(jnp.sum(p, axis=-1, keepdims=True) + jnp.exp(sink - m))
                probs.append(p.astype(BF16))
            pv = jnp.dot(jnp.concatenate(probs, axis=0), vd, preferred_element_type=F32)
            outs = [pv[g * blk:(g + 1) * blk] / denoms[g] for g in range(4)]
            o_ref[rows, (2 * h) * LANES:(2 * h + 1) * LANES] = jnp.where(low_half, outs[0], outs[1]).astype(BF16)
            o_ref[rows, (2 * h + 1) * LANES:(2 * h + 2) * LANES] = jnp.where(low_half, outs[2], outs[3]).astype(BF16)

        vg = zv_ref[rows, :]
        avg = avg_ref[...]
        mean = jnp.dot(vg, avg, preferred_element_type=F32)
        d = vg.astype(F32) - mean
        d2 = d * d
        d2_hi = d2.astype(BF16)
        d2_lo = (d2 - d2_hi.astype(F32)).astype(BF16)
        var = jnp.dot(d2_hi, avg, preferred_element_type=F32) + jnp.dot(d2_lo, avg, preferred_element_type=F32)
        vn = (d * lax.rsqrt(var + LN_EPS) * sg_ref[...] + sb_ref[...]).astype(BF16)
        for m_ in range(GATE_GROUPS // 2):
            cols = slice(m_ * LANES, (m_ + 1) * LANES)
            rhs = vn[:, cols]
            a = jnp.dot(ws_ref[2 * m_], rhs, preferred_element_type=F32)
            b = jnp.dot(ws_ref[2 * m_ + 1], rhs, preferred_element_type=F32)
            sv = jnp.where(low_half, a, b) + bias_ref[:, cols]
            o_ref[rows, Q_WIDTH + m_ * LANES:Q_WIDTH + (m_ + 1) * LANES] = (
                u_ref[rows, cols].astype(F32) * sv).astype(BF16)


def _mixer(q, k, v, u, zv, sink, avg, sgu_g, sgu_b, ws, bias, seq, tq):
    T = q.shape[0]
    r = tq // WINDOW_BLOCK
    tiles_per_seq = seq // tq
    blocks_per_seq = seq // WINDOW_BLOCK

    def prev_map(i, sink_ref):
        first = (i // tiles_per_seq) * blocks_per_seq
        return (jnp.maximum(i * r - 1, first), 0)

    def next_map(i, sink_ref):
        last = (i // tiles_per_seq + 1) * blocks_per_seq - 1
        return (jnp.minimum((i + 1) * r, last), 0)

    row = lambda i, sink_ref: (i, 0)
    const2 = lambda i, sink_ref: (0, 0)
    nb_spec_p = pl.BlockSpec((WINDOW_BLOCK, KV_DUP_WIDTH), prev_map)
    nb_spec_n = pl.BlockSpec((WINDOW_BLOCK, KV_DUP_WIDTH), next_map)
    cur_spec = pl.BlockSpec((tq, KV_DUP_WIDTH), row)
    grid_spec = pltpu.PrefetchScalarGridSpec(
        num_scalar_prefetch=1,
        grid=(T // tq,),
        in_specs=[
            pl.BlockSpec((tq, Q_WIDTH), row),
            nb_spec_p, cur_spec, nb_spec_n,
            nb_spec_p, cur_spec, nb_spec_n,
            pl.BlockSpec((tq, GATE_WIDTH), row),
            pl.BlockSpec((tq, GATE_WIDTH), row),
            pl.BlockSpec((GATE_WIDTH, GATE_WIDTH), const2),
            pl.BlockSpec((1, GATE_WIDTH), const2),
            pl.BlockSpec((1, GATE_WIDTH), const2),
            pl.BlockSpec((GATE_GROUPS, WINDOW_BLOCK, WINDOW_BLOCK), lambda i, sink_ref: (0, 0, 0)),
            pl.BlockSpec((WINDOW_BLOCK, GATE_WIDTH), const2),
        ],
        out_specs=pl.BlockSpec((tq, Q_WIDTH + GATE_WIDTH), row),
        scratch_shapes=[
            pltpu.VMEM((tq + 2 * WINDOW_BLOCK, KV_DUP_WIDTH), BF16),
            pltpu.VMEM((tq + 2 * WINDOW_BLOCK, KV_DUP_WIDTH), BF16),
        ],
    )
    return pl.pallas_call(
        functools.partial(_mixer_kernel, tiles_per_seq=tiles_per_seq),
        grid_spec=grid_spec,
        out_shape=jax.ShapeDtypeStruct((T, Q_WIDTH + GATE_WIDTH), BF16),
        compiler_params=_cparams(("parallel",)),
        name="window_attn_spatial_gate",
    )(sink, q, k, k, k, v, v, v, u, zv, avg, sgu_g, sgu_b, ws, bias)


def _proj_ln_kernel(a_ref, x_ref, w_ref, g_ref, b_ref, o_ref, *, alpha):
    y = jnp.dot(a_ref[...], w_ref[...], preferred_element_type=F32)
    o_ref[...] = _layer_norm(alpha * x_ref[...] + y, g_ref[...], b_ref[...])


def _proj_ln(a, x2d, w, ln_g, ln_b, alpha):
    T, D = x2d.shape
    K = a.shape[1]
    tm = TOKEN_TILE
    row = lambda i: (i, 0)
    const = lambda i: (0, 0)
    return pl.pallas_call(
        functools.partial(_proj_ln_kernel, alpha=alpha),
        grid=(T // tm,),
        in_specs=[pl.BlockSpec((tm, K), row), pl.BlockSpec((tm, D), row), pl.BlockSpec((K, D), const),
                  pl.BlockSpec((1, D), const), pl.BlockSpec((1, D), const)],
        out_specs=pl.BlockSpec((tm, D), row),
        out_shape=jax.ShapeDtypeStruct((T, D), F32),
        compiler_params=_cparams(("parallel",)),
        name="proj_residual_ln",
    )(a, x2d, w, ln_g, ln_b)


def _swiglu_ln_kernel(x_ref, wg_ref, wu_ref, wd_ref, g_ref, b_ref, o_ref, xb, acc, *, alpha):
    f = pl.program_id(1)

    @pl.when(f == 0)
    def _():
        xb[...] = x_ref[...].astype(BF16)

    xv = xb[...]
    gate = jnp.dot(xv, wg_ref[...], preferred_element_type=F32)
    up = jnp.dot(xv, wu_ref[...], preferred_element_type=F32)
    h = (jax.nn.silu(gate) * up).astype(BF16)
    part = jnp.dot(h, wd_ref[...], preferred_element_type=F32)

    @pl.when(f == 0)
    def _():
        acc[...] = part

    @pl.when(f > 0)
    def _():
        acc[...] += part

    @pl.when(f == pl.num_programs(1) - 1)
    def _():
        o_ref[...] = _layer_norm(alpha * x_ref[...] + acc[...], g_ref[...], b_ref[...])


def _swiglu_ln(x2d, wg, wu, wd, ln_g, ln_b, alpha, tm, tf):
    T, D = x2d.shape
    FF = wg.shape[1]
    return pl.pallas_call(
        functools.partial(_swiglu_ln_kernel, alpha=alpha),
        grid=(T // tm, FF // tf),
        in_specs=[
            pl.BlockSpec((tm, D), lambda i, f: (i, 0)),
            pl.BlockSpec((D, tf), lambda i, f: (0, f)),
            pl.BlockSpec((D, tf), lambda i, f: (0, f)),
            pl.BlockSpec((tf, D), lambda i, f: (f, 0)),
            pl.BlockSpec((1, D), lambda i, f: (0, 0)),
            pl.BlockSpec((1, D), lambda i, f: (0, 0)),
        ],
        out_specs=pl.BlockSpec((tm, D), lambda i, f: (i, 0)),
        out_shape=jax.ShapeDtypeStruct((T, D), F32),
        scratch_shapes=[pltpu.VMEM((tm, D), BF16), pltpu.VMEM((tm, D), F32)],
        compiler_params=_cparams(("parallel", "arbitrary")),
        name="swiglu_residual_ln",
    )(x2d, wg, wu, wd, ln_g, ln_b)


def _lane_block_transpose(v, lane):
    for k in range(3):
        m = 1 << k
        high = ((lane >> (4 + k)) & 1) == 1
        nxt = list(v)
        for i in range(SUBLANES):
            if i & m == 0:
                a, b = v[i], v[i + m]
                nxt[i] = jnp.where(high, pltpu.roll(b, S5_GROUP * m, 1), a)
                nxt[i + m] = jnp.where(high, b, pltpu.roll(a, LANES - S5_GROUP * m, 1))
        v = nxt
    return v


def _glu_chunks_ln_kernel(y_ref, x_ref, d_ref, wv_ref, wg_ref, g_ref, b_ref, o_ref, ybuf, *, alpha):
    lane = lax.broadcasted_iota(jnp.int32, (SUBLANES, LANES), 1)
    n_seq, L, D = x_ref.shape
    for o in range(D // LANES):
        cols = slice(o * LANES, (o + 1) * LANES)
        for th in range(L // SUBLANES):
            z = [y_ref[o * SUBLANES + s, :, th * LANES:(th + 1) * LANES] for s in range(SUBLANES)]
            v = _lane_block_transpose(z, lane)
            for tl in range(SUBLANES):
                ybuf[:, th * SUBLANES + tl, cols] = v[tl]
    x = x_ref[...].reshape(n_seq * L, D)
    a = _gelu(ybuf[...].reshape(n_seq * L, D) + d_ref[...] * x).astype(BF16)
    val = jnp.dot(a, wv_ref[...], preferred_element_type=F32)
    gate = jnp.dot(a, wg_ref[...], preferred_element_type=F32)
    out = _layer_norm(alpha * x + val * jax.nn.sigmoid(gate), g_ref[...], b_ref[...])
    o_ref[...] = out.reshape(n_seq, L, D)


def _glu_chunks_ln(y4, x3d, d_skip, wv, wg, ln_g, ln_b, alpha):
    Bsz, S, D = x3d.shape
    G, n, _, W = y4.shape
    L = S5_CHUNK
    tile = lambda b, j: (b, j, 0)
    const = lambda b, j: (0, 0)
    return pl.pallas_call(
        functools.partial(_glu_chunks_ln_kernel, alpha=alpha),
        grid=(Bsz // SUBLANES, n),
        in_specs=[pl.BlockSpec((G, None, SUBLANES, W), lambda b, j: (0, j, b, 0)),
                  pl.BlockSpec((SUBLANES, L, D), tile),
                  pl.BlockSpec((1, D), const),
                  pl.BlockSpec((D, D), const), pl.BlockSpec((D, D), const),
                  pl.BlockSpec((1, D), const), pl.BlockSpec((1, D), const)],
        out_specs=pl.BlockSpec((SUBLANES, L, D), tile),
        out_shape=jax.ShapeDtypeStruct((Bsz, S, D), F32),
        scratch_shapes=[pltpu.VMEM((SUBLANES, L, D), F32)],
        compiler_params=_cparams(("parallel", "parallel")),
        name="glu_residual_ln",
    )(y4, x3d, d_skip.reshape(1, D), wv, wg, ln_g, ln_b)


def _s5_kernel(u_ref, wt_ref, ws_ref, wo_ref, are_ref, aim_ref, o_ref, gbuf, sbuf, *, n_chunks, bsz):
    P2 = 2 * S5_STATE
    ub = u_ref[...]
    gbuf[...] = jnp.dot(ub, ws_ref[...], preferred_element_type=F32)
    a_re = are_ref[...]
    a_im = aim_ref[...]
    lane = lax.broadcasted_iota(jnp.int32, (bsz, P2), 1)
    fwd = lane < S5_STATE
    s_re = jnp.zeros((bsz, P2), F32)
    s_im = jnp.zeros((bsz, P2), F32)
    for i in range(n_chunks):
        if i > 0:
            rf = slice((i - 1) * bsz, i * bsz)
            rb = slice((n_chunks - i) * bsz, (n_chunks - i + 1) * bsz)
            g_re = jnp.where(fwd, gbuf[rf, 0:P2], gbuf[rb, 0:P2])
            g_im = jnp.where(fwd, gbuf[rf, P2:2 * P2], gbuf[rb, P2:2 * P2])
            s_re, s_im = (a_re * s_re - a_im * s_im + g_re, a_re * s_im + a_im * s_re + g_im)
        of = slice(i * bsz, (i + 1) * bsz)
        ob = slice((n_chunks - 1 - i) * bsz, (n_chunks - i) * bsz)
        sbuf[of, 0:S5_STATE] = s_re[:, 0:S5_STATE]
        sbuf[ob, S5_STATE:P2] = s_re[:, S5_STATE:P2]
        sbuf[of, P2:P2 + S5_STATE] = s_im[:, 0:S5_STATE]
        sbuf[ob, P2 + S5_STATE:2 * P2] = s_im[:, S5_STATE:P2]
    y = jnp.dot(ub, wt_ref[...], preferred_element_type=F32)
    o_ref[...] = y + jnp.dot(sbuf[...].astype(BF16), wo_ref[...], preferred_element_type=F32)


def _s5_tables(lam_re, lam_im, log_dt, b_re, b_im, c_re, c_im):
    L = S5_CHUNK
    hp = lax.Precision.HIGHEST
    G, P = lam_re.shape[1], lam_re.shape[2]
    C = b_re.shape[-1]
    dt = jnp.exp(log_dt)[..., None]
    zr, zi = lam_re * dt, lam_im * dt
    er = jnp.exp(zr)
    lbr, lbi = er * jnp.cos(zi), er * jnp.sin(zi)
    nr, ni = lbr - 1.0, lbi
    den = lam_re * lam_re + lam_im * lam_im
    cr = (nr * lam_re + ni * lam_im) / den
    ci = (ni * lam_re - nr * lam_im) / den
    bbr = cr[..., None] * b_re - ci[..., None] * b_im
    bbi = cr[..., None] * b_im + ci[..., None] * b_re
    k = jnp.arange(L + 1, dtype=F32)[:, None, None, None]
    mag = jnp.exp(k * zr[None])
    pwr, pwi = mag * jnp.cos(k * zi[None]), mag * jnp.sin(k * zi[None])

    cbr = c_re[..., None] * bbr[:, :, None] - c_im[..., None] * bbi[:, :, None]
    cbi = c_re[..., None] * bbi[:, :, None] + c_im[..., None] * bbr[:, :, None]
    kern = (jnp.einsum('kzgp,zgcpd->zgkcd', pwr[:L], cbr, precision=hp)
            - jnp.einsum('kzgp,zgcpd->zgkcd', pwi[:L], cbi, precision=hp))
    kf, kb = kern[0], kern[1]
    kfull = jnp.concatenate([kb[:, 1:][:, ::-1], (kf[:, 0] + kb[:, 0])[:, None], kf[:, 1:]], axis=1)
    by_in = kfull.transpose(0, 3, 1, 2).astype(BF16)
    window = lambda start: lax.dynamic_slice_in_dim(by_in, start, L, axis=2)
    w_toep = jax.vmap(window, out_axes=1)(L - 1 - jnp.arange(L)).reshape(G, L * C, L * C)

    pf_r, pf_i = pwr[:L, 0][::-1], pwi[:L, 0][::-1]
    pb_r, pb_i = pwr[:L, 1], pwi[:L, 1]

    def state_cols(pr, pi, br, bi):
        re = pr[..., None] * br[None] - pi[..., None] * bi[None]
        im = pr[..., None] * bi[None] + pi[..., None] * br[None]
        return re.transpose(1, 0, 3, 2), im.transpose(1, 0, 3, 2)

    sf_r, sf_i = state_cols(pf_r, pf_i, bbr[0], bbi[0])
    sb_r, sb_i = state_cols(pb_r, pb_i, bbr[1], bbi[1])
    w_state = jnp.concatenate([sf_r, sb_r, sf_i, sb_i], axis=-1).reshape(G, L * C, 4 * P)

    def out_rows(pr, pi, cre, cim):
        er_ = cre[None] * pr[:, :, None] - cim[None] * pi[:, :, None]
        ei_ = cre[None] * pi[:, :, None] + cim[None] * pr[:, :, None]
        return er_.transpose(1, 3, 0, 2), -ei_.transpose(1, 3, 0, 2)

    of_r, of_i = out_rows(pwr[1:L + 1, 0], pwi[1:L + 1, 0], c_re[0], c_im[0])
    ob_r, ob_i = out_rows(pwr[1:L + 1, 1][::-1], pwi[1:L + 1, 1][::-1], c_re[1], c_im[1])
    w_out = jnp.concatenate([of_r, ob_r, of_i, ob_i], axis=1).reshape(G, 4 * P, L * C)

    a_re = jnp.concatenate([pwr[L, 0], pwr[L, 1]], axis=-1)[:, None, :]
    a_im = jnp.concatenate([pwi[L, 0], pwi[L, 1]], axis=-1)[:, None, :]
    return w_toep, w_state.astype(BF16), w_out.astype(BF16), a_re, a_im


def _s5(x3d, tables):
    w_toep, w_state, w_out, a_re, a_im = tables
    Bsz, S, D = x3d.shape
    L, C = S5_CHUNK, S5_GROUP
    G = D // C
    n = S // L
    R = n * Bsz
    W = L * C
    u = x3d.astype(BF16).reshape(Bsz, n, L, G, C).transpose(3, 1, 0, 2, 4).reshape(G, R, W)
    grp = lambda g: (g, 0, 0)
    y = pl.pallas_call(
        functools.partial(_s5_kernel, n_chunks=n, bsz=Bsz),
        grid=(G,),
        in_specs=[
            pl.BlockSpec((None, R, W), grp),
            pl.BlockSpec((None, W, W), grp),
            pl.BlockSpec((None, W, 4 * S5_STATE), grp),
            pl.BlockSpec((None, 4 * S5_STATE, W), grp),
            pl.BlockSpec((None, 1, 2 * S5_STATE), grp),
            pl.BlockSpec((None, 1, 2 * S5_STATE), grp),
        ],
        out_specs=pl.BlockSpec((None, R, W), grp),
        out_shape=jax.ShapeDtypeStruct((G, R, W), F32),
        scratch_shapes=[pltpu.VMEM((R, 4 * S5_STATE), F32), pltpu.VMEM((R, 4 * S5_STATE), F32)],
        compiler_params=_cparams(("parallel",)),
        name="s5_chunked_conv",
    )(u, w_toep, w_state, w_out, a_re, a_im)
    return y.reshape(G, n, Bsz, W)


def _router_kernel(x_ref, wh_ref, wl_ref, o_ref):
    x = x_ref[...]
    xh = x.astype(BF16)
    xl = (x - xh.astype(F32)).astype(BF16)
    wh = wh_ref[...]
    logits = (jnp.dot(xh, wh, preferred_element_type=F32) + jnp.dot(xl, wh, preferred_element_type=F32)
              + jnp.dot(xh, wl_ref[...], preferred_element_type=F32))
    lane = lax.broadcasted_iota(jnp.int32, logits.shape, 1)
    logits = jnp.where(lane < N_EXPERTS, logits, -jnp.inf)
    v1 = jnp.max(logits, axis=-1, keepdims=True)
    lane_f = lane.astype(F32)
    i1 = jnp.min(jnp.where(logits == v1, lane_f, float(LANES)), axis=-1, keepdims=True)
    rest = jnp.where(lane_f == i1, -jnp.inf, logits)
    v2 = jnp.max(rest, axis=-1, keepdims=True)
    i2 = jnp.min(jnp.where(rest == v2, lane_f, float(LANES)), axis=-1, keepdims=True)
    e2 = jnp.exp(v2 - v1)
    den = 1.0 + e2
    w1 = 1.0 / den
    w2 = e2 / den
    o_ref[...] = jnp.where(lane == 0, w1, jnp.where(lane == 1, w2, jnp.where(
        lane == 2, i1, jnp.where(lane == 3, i2, 0.0))))


def _router(x2d, router_w, tm):
    T, D = x2d.shape
    wpad = jnp.zeros((D, LANES), F32).at[:, :N_EXPERTS].set(router_w)
    wh = wpad.astype(BF16)
    wl = (wpad - wh.astype(F32)).astype(BF16)
    return pl.pallas_call(
        _router_kernel,
        grid=(T // tm,),
        in_specs=[pl.BlockSpec((tm, D), lambda i: (i, 0)),
                  pl.BlockSpec((D, LANES), lambda i: (0, 0)),
                  pl.BlockSpec((D, LANES), lambda i: (0, 0))],
        out_specs=pl.BlockSpec((tm, LANES), lambda i: (i, 0)),
        out_shape=jax.ShapeDtypeStruct((T, LANES), F32),
        compiler_params=_cparams(("parallel",)),
        name="router_top2",
    )(x2d, wh, wl)


def _route_plan(info, tm):
    T = info.shape[0]
    n_pairs = 2 * T
    n_tiles = n_pairs // tm + N_EXPERTS
    n_slots = n_tiles * tm
    gate_w = info[:, 0:2].T.reshape(-1)
    expert = info[:, 2:4].astype(jnp.int32).T.reshape(-1)
    order = jnp.sort(expert * n_pairs + jnp.arange(n_pairs, dtype=jnp.int32)) % n_pairs
    counts = jnp.sum(expert[:, None] == jnp.arange(N_EXPERTS, dtype=jnp.int32)[None, :], axis=0, dtype=jnp.int32)
    padded = ((counts + tm - 1) // tm) * tm
    pad_end = jnp.cumsum(padded)
    pad_start = pad_end - padded
    cmp_start = jnp.cumsum(counts) - counts
    tile_start = jnp.arange(n_tiles, dtype=jnp.int32) * tm
    tile_expert = jnp.minimum(jnp.sum(tile_start[:, None] >= pad_end[None, :], axis=1, dtype=jnp.int32),
                              N_EXPERTS - 1)
    n_valid = jnp.clip(counts[tile_expert] - (tile_start - pad_start[tile_expert]), 0, tm)
    n_valid = jnp.where(tile_start < pad_end[-1], n_valid, 0).astype(jnp.int32)
    slot = jnp.arange(n_slots, dtype=jnp.int32)
    s_exp = jnp.repeat(tile_expert, tm)
    rank = slot - pad_start[s_exp]
    valid = (rank < counts[s_exp]) & (slot < pad_end[-1])
    pair = order[jnp.clip(cmp_start[s_exp] + rank, 0, n_pairs - 1)]
    slot_tok = jnp.where(valid, pair % T, 0)
    slot_dst = jnp.where(valid, pair, 0)
    slot_w = jnp.where(valid, gate_w[pair], 0.0)
    return (tile_expert, n_valid, slot_tok.reshape(n_tiles, 1, tm), slot_dst.reshape(n_tiles, 1, tm),
            slot_w.reshape(n_slots, 1), n_tiles)


def _moe_kernel(texp_ref, nvalid_ref, tok0_ref, tokn_ref, dst_ref, x_hbm, sw_ref, wg_ref, wu_ref, wd_ref,
                out_hbm, xbuf, xb, acc, obuf, sem_in, sem_out):
    i = pl.program_id(0)
    f = pl.program_id(1)
    n_f = pl.num_programs(1)
    n_tiles = pl.num_programs(0)
    tm = xb.shape[0]
    rows_here = nvalid_ref[i]
    rows_prev = nvalid_ref[jnp.maximum(i - 1, 0)]
    active = rows_here > 0
    next_active = jnp.logical_and(i + 1 < n_tiles, nvalid_ref[jnp.minimum(i + 1, n_tiles - 1)] > 0)
    slot = lax.rem(i, 2)

    def gather_copy(tok_ref, r, buf_slot):
        return pltpu.make_async_copy(x_hbm.at[pl.ds(tok_ref[0, 0, r], 1), :],
                                     xbuf.at[buf_slot, pl.ds(r, 1), :], sem_in.at[buf_slot])

    def scatter_copy(r):
        return pltpu.make_async_copy(obuf.at[pl.ds(r, 1), :],
                                     out_hbm.at[pl.ds(dst_ref[0, 0, r], 1), :], sem_out.at[0])

    def row_wait_in(buf_slot):
        pltpu.make_async_copy(x_hbm.at[pl.ds(0, 1), :], xbuf.at[buf_slot, pl.ds(0, 1), :],
                              sem_in.at[buf_slot]).wait()

    def row_wait_out():
        pltpu.make_async_copy(obuf.at[pl.ds(0, 1), :], out_hbm.at[pl.ds(0, 1), :], sem_out.at[0]).wait()

    def blocked_loop(n_rows, unroll, fn):
        n_blocks = lax.shift_right_logical(n_rows, unroll.bit_length() - 1)

        def block(k, c):
            for j in range(unroll):
                fn(k * unroll + j, j)
            return c
        lax.fori_loop(0, n_blocks, block, 0)

        def single(r, c):
            fn(r, 0)
            return c
        lax.fori_loop(n_blocks * unroll, n_rows, single, 0)

    def start_gather(tok_ref, buf_slot):
        blocked_loop(jnp.int32(tm), DMA_START_UNROLL,
                     lambda r, j: gather_copy(tok_ref, r, buf_slot).start(priority=j % 2))

    def wait_gather(buf_slot):
        blocked_loop(jnp.int32(tm), DMA_WAIT_UNROLL, lambda r, j: row_wait_in(buf_slot))

    def wait_scatter(n_rows):
        blocked_loop(n_rows, DMA_WAIT_UNROLL, lambda r, j: row_wait_out())

    @pl.when(jnp.logical_and(f == 0, i == 0))
    def _():
        start_gather(tok0_ref, 0)

    @pl.when(jnp.logical_and(f == 0, active))
    def _():
        wait_gather(slot)
        xb[...] = xbuf[slot].astype(BF16)

        @pl.when(next_active)
        def _():
            start_gather(tokn_ref, 1 - slot)

    @pl.when(active)
    def _():
        xv = xb[...]
        gate = jnp.dot(xv, wg_ref[...], preferred_element_type=F32)
        up = jnp.dot(xv, wu_ref[...], preferred_element_type=F32)
        h = (jax.nn.silu(gate) * up).astype(BF16)
        part = jnp.dot(h, wd_ref[...], preferred_element_type=F32)

        @pl.when(f == 0)
        def _():
            acc[...] = part

        @pl.when(f > 0)
        def _():
            acc[...] += part

    @pl.when(jnp.logical_and(f == n_f - 1, active))
    def _():
        @pl.when(i > 0)
        def _():
            wait_scatter(rows_prev)

        obuf[...] = acc[...] * sw_ref[...]

        blocked_loop(rows_here, DMA_START_UNROLL, lambda r, j: scatter_copy(r).start(priority=j % 2))

        @pl.when(jnp.logical_not(next_active))
        def _():
            wait_scatter(rows_here)


def _moe_experts(x2d, plan, wg, wu, wd, tm, tf):
    tile_expert, n_valid, slot_tok, slot_dst, slot_w, n_tiles = plan
    T, D = x2d.shape
    FF = wg.shape[2]
    smem_blk = lambda imap: pl.BlockSpec((1, 1, tm), imap, memory_space=pltpu.SMEM)
    grid_spec = pltpu.PrefetchScalarGridSpec(
        num_scalar_prefetch=2,
        grid=(n_tiles, FF // tf),
        in_specs=[
            smem_blk(lambda i, f, te, nu: (0, 0, 0)),
            smem_blk(lambda i, f, te, nu: (jnp.minimum(i + 1, n_tiles - 1), 0, 0)),
            smem_blk(lambda i, f, te, nu: (i, 0, 0)),
            pl.BlockSpec(memory_space=pl.ANY),
            pl.BlockSpec((tm, 1), lambda i, f, te, nu: (i, 0)),
            pl.BlockSpec((None, D, tf), lambda i, f, te, nu: (te[i], 0, f)),
            pl.BlockSpec((None, D, tf), lambda i, f, te, nu: (te[i], 0, f)),
            pl.BlockSpec((None, tf, D), lambda i, f, te, nu: (te[i], f, 0)),
        ],
        out_specs=pl.BlockSpec(memory_space=pl.ANY),
        scratch_shapes=[
            pltpu.VMEM((2, tm, D), F32),
            pltpu.VMEM((tm, D), BF16),
            pltpu.VMEM((tm, D), F32),
            pltpu.VMEM((tm, D), F32),
            pltpu.SemaphoreType.DMA((2,)),
            pltpu.SemaphoreType.DMA((1,)),
        ],
    )
    return pl.pallas_call(
        _moe_kernel,
        grid_spec=grid_spec,
        out_shape=jax.ShapeDtypeStruct((2 * T, D), F32),
        compiler_params=_cparams(("arbitrary", "arbitrary")),
        name="moe_grouped_swiglu",
    )(tile_expert, n_valid, slot_tok, slot_tok, slot_dst, x2d, slot_w, wg, wu, wd)


def _combine_ln_kernel(x_ref, y0_ref, y1_ref, g_ref, b_ref, o_ref, *, alpha):
    o_ref[...] = _layer_norm(alpha * x_ref[...] + (y0_ref[...] + y1_ref[...]), g_ref[...], b_ref[...])


def _combine_ln(x2d, y, ln_g, ln_b, alpha, tm):
    T, D = x2d.shape
    nt = T // tm
    return pl.pallas_call(
        functools.partial(_combine_ln_kernel, alpha=alpha),
        grid=(nt,),
        in_specs=[pl.BlockSpec((tm, D), lambda i: (i, 0)),
                  pl.BlockSpec((tm, D), lambda i: (i, 0)),
                  pl.BlockSpec((tm, D), lambda i: (i + nt, 0)),
                  pl.BlockSpec((1, D), lambda i: (0, 0)),
                  pl.BlockSpec((1, D), lambda i: (0, 0))],
        out_specs=pl.BlockSpec((tm, D), lambda i: (i, 0)),
        out_shape=jax.ShapeDtypeStruct((T, D), F32),
        compiler_params=_cparams(("parallel",)),
        name="moe_combine_ln",
    )(x2d, y, y, ln_g, ln_b)


def _rope_tables(seq):
    pos = jnp.arange(seq, dtype=F32)
    inv_freq = ROPE_THETA ** (-jnp.arange(0, HEAD_DIM, 2, dtype=F32) / HEAD_DIM)
    ang = pos[:, None] * inv_freq[None, :]
    cos, sin = jnp.cos(ang), jnp.sin(ang)
    return jnp.tile(cos, (1, 4)), jnp.tile(jnp.concatenate([-sin, sin], axis=1), (1, 2))


def kernel(x, ln_g, ln_b, mix_w_in, attn_sink, sgu_w, sgu_b, sgu_ln_g, sgu_ln_b, mix_w_out, ffn_w_gate, ffn_w_up, ffn_w_down, s5_lam_re, s5_lam_im, s5_log_dt, s5_b_re, s5_b_im, s5_c_re, s5_c_im, s5_d, glu_w_val, glu_w_gate, router_w, moe_w_gate, moe_w_up, moe_w_down):
    Bsz, S, D = x.shape
    depth = ln_g.shape[0]
    T = Bsz * S
    alpha = (2.0 * depth) ** 0.25
    tm = TOKEN_TILE
    cos_t, sin_t = _rope_tables(S)
    avg = jnp.kron(jnp.eye(GATE_GROUPS, dtype=F32), jnp.full((HEAD_DIM, HEAD_DIM), 1.0 / HEAD_DIM, F32)).astype(BF16)
    x2d = x.reshape(T, D)
    for layer in range(depth):
        i = layer // 2
        g0, b0 = ln_g[layer, 0][None, :], ln_b[layer, 0][None, :]
        g1, b1 = ln_g[layer, 1][None, :], ln_b[layer, 1][None, :]
        if layer % 2 == 0:
            w = mix_w_in[i]
            wq, wk, wv, wz = (w[:, :Q_WIDTH], w[:, Q_WIDTH:Q_WIDTH + 128], w[:, Q_WIDTH + 128:Q_WIDTH + 256],
                              w[:, Q_WIDTH + 256:])
            dup = lambda m: jnp.concatenate([m[:, :HEAD_DIM], m[:, :HEAD_DIM], m[:, HEAD_DIM:], m[:, HEAD_DIM:]], axis=1)
            w_ext = jnp.concatenate([wq, dup(wk), dup(wv), wz], axis=1).astype(BF16)
            q, k, v, u, zv = _inproj(x2d, w_ext, cos_t, sin_t, S, tm)
            bias = jnp.repeat(sgu_b[i].T, HEAD_DIM, axis=1)
            mixed = _mixer(q, k, v, u, zv, attn_sink[i], avg, sgu_ln_g[i].reshape(1, -1),
                           sgu_ln_b[i].reshape(1, -1), sgu_w[i].astype(BF16), bias, S, tm)
            x2d = _proj_ln(mixed, x2d, mix_w_out[i].astype(BF16), g0, b0, alpha)
            x2d = _swiglu_ln(x2d, ffn_w_gate[i].astype(BF16), ffn_w_up[i].astype(BF16),
                             ffn_w_down[i].astype(BF16), g1, b1, alpha, tm, FF_TILE)
        else:
            tables = _s5_tables(s5_lam_re[i], s5_lam_im[i], s5_log_dt[i], s5_b_re[i], s5_b_im[i],
                                s5_c_re[i], s5_c_im[i])
            y = _s5(x2d.reshape(Bsz, S, D), tables)
            x2d = _glu_chunks_ln(y, x2d.reshape(Bsz, S, D), s5_d[i], glu_w_val[i].astype(BF16),
                                 glu_w_gate[i].astype(BF16), g0, b0, alpha).reshape(T, D)
            info = _router(x2d, router_w[i], tm)
            plan = _route_plan(info, tm)
            y = _moe_experts(x2d, plan, moe_w_gate[i].astype(BF16), moe_w_up[i].astype(BF16),
                             moe_w_down[i].astype(BF16), tm, FF_TILE)
            x2d = _combine_ln(x2d, y, g1, b1, alpha, tm)
    return x2d.reshape(Bsz, S, D)
```
